```python
import jax, jax.numpy as jnp
from jax import lax
import numpy as np

D_MODEL = 1024
BATCH = 4
SEQ = 4096
DEPTH = 2

GRID_W = 64
CTX_LEN = 256
HEAD_DIM = 64
N_Q_HEADS = 8
N_KV_HEADS = 2
GQA_GROUP = N_Q_HEADS // N_KV_HEADS
ATTN_WIDTH = N_Q_HEADS * HEAD_DIM
KV_WIDTH = N_KV_HEADS * HEAD_DIM
CONV_GROUPS = 4
CONV_WIDTH = CONV_GROUPS * 64
CONV_K = 3
FOURIER_GROUPS = 4
FOURIER_GROUP_DIM = 64
FOURIER_WIDTH = FOURIER_GROUPS * FOURIER_GROUP_DIM
N_BRANCHES = 3
D_FF = 4 * D_MODEL
Q_BLOCK = 128
ROPE_THETA = 10000.0
ROPE_HALF = HEAD_DIM // 2
EPS = 1e-6
N_MOD = 6

OFF_B = 0
OFF_C = OFF_B + CONV_WIDTH
OFF_X = OFF_C + CONV_WIDTH
OFF_F = OFF_X + CONV_WIDTH
OFF_Q = OFF_F + FOURIER_WIDTH
OFF_K = OFF_Q + ATTN_WIDTH
OFF_V = OFF_K + KV_WIDTH
OFF_G = OFF_V + KV_WIDTH
IN_WIDTH = OFF_G + N_BRANCHES * D_MODEL

kernel_name = "hybrid_conv_fourier_gqa_diffusion_block"


def rms_norm(x, g):
    xf = x.astype(jnp.float32)
    y = xf * lax.rsqrt(jnp.mean(xf * xf, axis=-1, keepdims=True) + EPS)
    return (y * g.astype(jnp.float32)).astype(x.dtype)


def modulate(x, g, shift, scale):
    return rms_norm(x, g) * (1 + scale) + shift


def axial_rope_tables(rows):
    n_freq = ROPE_HALF // 2
    inv = ROPE_THETA ** (-jnp.arange(n_freq, dtype=jnp.float32) / n_freq)
    row_ang = jnp.repeat(jnp.arange(rows, dtype=jnp.float32)[:, None] * inv, GRID_W, axis=0)
    col_ang = jnp.tile(jnp.arange(GRID_W, dtype=jnp.float32)[:, None] * inv, (rows, 1))
    ang = jnp.concatenate([row_ang, col_ang], axis=-1)
    return jnp.cos(ang), jnp.sin(ang)


def apply_rope(x, cos, sin):
    xf = x.astype(jnp.float32)
    x1, x2 = xf[..., :ROPE_HALF], xf[..., ROPE_HALF:]
    c = cos[None, :, None, :]
    s = sin[None, :, None, :]
    return jnp.concatenate([x1 * c - x2 * s, x1 * s + x2 * c], axis=-1).astype(x.dtype)


def short_conv(u, w):
    L = u.shape[1]
    pad = CONV_K // 2
    up = jnp.pad(u, ((0, 0), (pad, pad), (0, 0)))
    return sum(up[:, j:j + L] * w[j] for j in range(CONV_K))


def fourier_mix(u):
    b, L, _ = u.shape
    ug = u.astype(jnp.float32).reshape(b, L, FOURIER_GROUPS, FOURIER_GROUP_DIM)
    y = jnp.fft.fft2(ug, axes=(1, 3), norm="ortho").real
    return y.reshape(b, L, FOURIER_WIDTH).astype(u.dtype)


def local_branches(z, w_conv):
    y_conv = z[..., OFF_B:OFF_C] * short_conv(z[..., OFF_C:OFF_X] * z[..., OFF_X:OFF_F], w_conv)
    y_four = fourier_mix(z[..., OFF_F:OFF_Q])
    return y_conv, y_four


def qkv_heads(z, g_q, g_k):
    b, L, _ = z.shape
    q = rms_norm(z[..., OFF_Q:OFF_K].reshape(b, L, N_Q_HEADS, HEAD_DIM), g_q)
    k = rms_norm(z[..., OFF_K:OFF_V].reshape(b, L, N_KV_HEADS, HEAD_DIM), g_k)
    v = z[..., OFF_V:OFF_G].reshape(b, L, N_KV_HEADS, HEAD_DIM)
    return q, k, v


def gqa(q, k, v):
    b, lq = q.shape[:2]
    qg = q.reshape(b, lq, N_KV_HEADS, GQA_GROUP, HEAD_DIM)
    s = jnp.einsum('bqgrd,bkgd->bgrqk', qg, k).astype(jnp.float32) * (HEAD_DIM ** -0.5)
    p = jax.nn.softmax(s, axis=-1).astype(v.dtype)
    o = jnp.einsum('bgrqk,bkgd->bqgrd', p, v)
    return o.reshape(b, lq, ATTN_WIDTH)


def blocked_gqa(q, k, v):
    b, lq = q.shape[:2]
    nb = lq // Q_BLOCK
    qb = q.reshape(b, nb, Q_BLOCK, N_Q_HEADS, HEAD_DIM).transpose(1, 0, 2, 3, 4)
    ob = lax.map(lambda qi: gqa(qi, k, v), qb)
    return ob.transpose(1, 0, 2, 3).reshape(b, lq, ATTN_WIDTH)


def merge_branches(z, y_conv, y_four, y_attn, w_conv_out, w_four_out, w_attn_out, w_o):
    gates = jax.nn.sigmoid(z[..., OFF_G:].astype(jnp.float32)).astype(z.dtype)
    g_conv, g_four, g_attn = jnp.split(gates, N_BRANCHES, axis=-1)
    m = g_conv * (y_conv @ w_conv_out) + g_four * (y_four @ w_four_out) + g_attn * (y_attn @ w_attn_out)
    return m @ w_o


def sq_relu_mlp(h, w1, w2):
    return jnp.square(jax.nn.relu(h @ w1)) @ w2


def setup_inputs(seed: int = 0) -> dict:
    key = jax.random.key(seed)
    ks = jax.random.split(key, 20)
    f32 = jnp.float32
    nrm = lambda k, shape, s: jax.random.normal(k, shape, f32) * s
    return {
        "x": nrm(ks[0], (BATCH, SEQ, D_MODEL), 1.0),
        "c": nrm(ks[1], (BATCH, D_MODEL), 1.0),
        "ctx": nrm(ks[2], (BATCH, CTX_LEN, D_MODEL), 1.0),
        "c_ctx": nrm(ks[3], (D_MODEL,), 1.0),
        "w_mod": nrm(ks[4], (DEPTH, D_MODEL, N_MOD * D_MODEL), 0.5 * D_MODEL ** -0.5),
        "b_mod": nrm(ks[5], (DEPTH, N_MOD * D_MODEL), 0.01),
        "g_norm1": 1.0 + nrm(ks[6], (DEPTH, D_MODEL), 0.05),
        "g_norm2": 1.0 + nrm(ks[7], (DEPTH, D_MODEL), 0.05),
        "w_in": nrm(ks[8], (DEPTH, D_MODEL, IN_WIDTH), D_MODEL ** -0.5),
        "w_conv": nrm(ks[9], (DEPTH, CONV_K, CONV_WIDTH), CONV_K ** -0.5),
        "g_q": 1.0 + nrm(ks[10], (DEPTH, HEAD_DIM), 0.05),
        "g_k": 1.0 + nrm(ks[11], (DEPTH, HEAD_DIM), 0.05),
        "w_conv_out": nrm(ks[12], (DEPTH, CONV_WIDTH, D_MODEL), CONV_WIDTH ** -0.5),
        "w_four_out": nrm(ks[13], (DEPTH, FOURIER_WIDTH, D_MODEL), FOURIER_WIDTH ** -0.5),
        "w_attn_out": nrm(ks[14], (DEPTH, ATTN_WIDTH, D_MODEL), ATTN_WIDTH ** -0.5),
        "w_o": nrm(ks[15], (DEPTH, D_MODEL, D_MODEL), D_MODEL ** -0.5),
        "w_ff1": nrm(ks[16], (DEPTH, D_MODEL, D_FF), D_MODEL ** -0.5),
        "w_ff2": nrm(ks[17], (DEPTH, D_FF, D_MODEL), D_FF ** -0.5),
    }


def reference(x, c, ctx, c_ctx, w_mod, b_mod, g_norm1, g_norm2, w_in, w_conv, g_q, g_k,
              w_conv_out, w_four_out, w_attn_out, w_o, w_ff1, w_ff2):
    S = x.shape[1]
    ROWS = S // GRID_W
    cos, sin = axial_rope_tables(ROWS)
    ctx_s = ctx
    for l in range(DEPTH):
        mod = (jax.nn.silu(c) @ w_mod[l] + b_mod[l])[:, None, :]
        mod_c = (jax.nn.silu(c_ctx)[None] @ w_mod[l] + b_mod[l])[:, None, :]
        sh1, sc1, gt1, sh2, sc2, gt2 = jnp.split(mod, N_MOD, axis=-1)
        csh1, csc1, cgt1, csh2, csc2, cgt2 = jnp.split(mod_c, N_MOD, axis=-1)

        hc = modulate(ctx_s, g_norm1[l], csh1, csc1)
        zc = hc @ w_in[l]
        qc, kc, vc = qkv_heads(zc, g_q[l], g_k[l])

        h = modulate(x, g_norm1[l], sh1, sc1)
        z = h @ w_in[l]
        q, k, v = qkv_heads(z, g_q[l], g_k[l])
        q = apply_rope(q, cos, sin)
        k = apply_rope(k, cos, sin)
        k_all = jnp.concatenate([k, kc], axis=1)
        v_all = jnp.concatenate([v, vc], axis=1)
        y_attn = blocked_gqa(q, k_all, v_all)
        y_conv, y_four = local_branches(z, w_conv[l])
        x = x + gt1 * merge_branches(z, y_conv, y_four, y_attn,
                                     w_conv_out[l], w_four_out[l], w_attn_out[l], w_o[l])
        x = x + gt2 * sq_relu_mlp(modulate(x, g_norm2[l], sh2, sc2), w_ff1[l], w_ff2[l])

        if l < DEPTH - 1:
            yc_attn = gqa(qc, kc, vc)
            yc_conv, yc_four = local_branches(zc, w_conv[l])
            ctx_s = ctx_s + cgt1 * merge_branches(zc, yc_conv, yc_four, yc_attn,
                                                  w_conv_out[l], w_four_out[l], w_attn_out[l], w_o[l])
            ctx_s = ctx_s + cgt2 * sq_relu_mlp(modulate(ctx_s, g_norm2[l], csh2, csc2),
                                               w_ff1[l], w_ff2[l])
    return x
```

```python
import functools

import jax
import jax.numpy as jnp
from jax import lax
from jax.experimental import pallas as pl
from jax.experimental.pallas import tpu as pltpu

D_MODEL = 1024
GRID_W = 64
HEAD_DIM = 64
N_Q_HEADS = 8
N_KV_HEADS = 2
GQA_GROUP = N_Q_HEADS // N_KV_HEADS
ATTN_WIDTH = N_Q_HEADS * HEAD_DIM
KV_WIDTH = N_KV_HEADS * HEAD_DIM
CONV_WIDTH = 256
CONV_K = 3
FOURIER_GROUPS = 4
FOURIER_GROUP_DIM = 64
FOURIER_WIDTH = FOURIER_GROUPS * FOURIER_GROUP_DIM
N_BRANCHES = 3
D_FF = 4 * D_MODEL
ROPE_THETA = 10000.0
ROPE_HALF = HEAD_DIM // 2
EPS = 1e-6
N_MOD = 6

OFF_B = 0
OFF_C = OFF_B + CONV_WIDTH
OFF_X = OFF_C + CONV_WIDTH
OFF_F = OFF_X + CONV_WIDTH
OFF_Q = OFF_F + FOURIER_WIDTH
OFF_K = OFF_Q + ATTN_WIDTH
OFF_V = OFF_K + KV_WIDTH
OFF_G = OFF_V + KV_WIDTH

LANES = 128
BF16_SUBLANES = 16
QK_WIDTH = ATTN_WIDTH + KV_WIDTH
MOD_ROWS = 8

BF16 = jnp.bfloat16
F32 = jnp.float32


def _dot(a, b):
    return jnp.dot(a, b, preferred_element_type=F32)


def _params(semantics, vmem_mb):
    return pltpu.CompilerParams(dimension_semantics=semantics, vmem_limit_bytes=vmem_mb << 20)


def _modulate(x, g, shift, scale):
    ms = jnp.mean(x * x, axis=-1, keepdims=True)
    y = x * lax.rsqrt(ms + EPS)
    return (y * g) * (1.0 + scale) + shift


def _sigmoid(x):
    return 1.0 / (1.0 + jnp.exp(-x))


def _mod_kernel(c_ref, w_ref, b_ref, o_ref):
    c = c_ref[...]
    s = (c * _sigmoid(c)).astype(BF16)
    o_ref[...] = _dot(s, w_ref[...].astype(BF16)) + b_ref[...]


def _mod_call(cc, w_mod, b_mod):
    depth = w_mod.shape[0]
    tn = 1024
    return pl.pallas_call(
        _mod_kernel,
        out_shape=jax.ShapeDtypeStruct((depth, MOD_ROWS, N_MOD * D_MODEL), F32),
        grid=(depth, N_MOD * D_MODEL // tn),
        in_specs=[
            pl.BlockSpec((MOD_ROWS, D_MODEL), lambda l, j: (0, 0)),
            pl.BlockSpec((None, D_MODEL, tn), lambda l, j: (l, 0, j)),
            pl.BlockSpec((None, 1, tn), lambda l, j: (l, 0, j)),
        ],
        out_specs=pl.BlockSpec((None, MOD_ROWS, tn), lambda l, j: (l, 0, j)),
        compiler_params=_params(("arbitrary", "arbitrary"), 32),
        name="mod",
    )(cc, w_mod, b_mod.reshape(depth, 1, N_MOD * D_MODEL))


def _in_proj_kernel(x_ref, mod_ref, g1_ref, w_ref, gqk_ref, cos_ref, sin_ref, w64_ref,
                    bz_ref, u_ref, fab_ref, qt_ref, k_ref, vt_ref):
    h = _modulate(x_ref[...], g1_ref[...], mod_ref[0:1, :], mod_ref[1:2, :]).astype(BF16)
    z = _dot(h, w_ref[...])
    bz_ref[...] = z[:, OFF_B:OFF_C].astype(BF16)
    u_ref[...] = (z[:, OFF_C:OFF_X] * z[:, OFF_X:OFF_F]).astype(BF16)
    fab_ref[...] = _dot(z[:, OFF_F:OFF_Q].astype(BF16), w64_ref[...]).astype(BF16)

    tm = z.shape[0]
    lane = lax.broadcasted_iota(jnp.int32, (tm, LANES), 1)
    head_lo = lane < HEAD_DIM
    half_lo = (lane % HEAD_DIM) < ROPE_HALF
    cos = cos_ref[...]
    sin = sin_ref[...]
    for j in range(QK_WIDTH // LANES):
        zj = z[:, OFF_Q + j * LANES:OFF_Q + (j + 1) * LANES]
        sq = zj * zj
        s_lo = jnp.sum(jnp.where(head_lo, sq, 0.0), axis=-1, keepdims=True)
        s_hi = jnp.sum(jnp.where(head_lo, 0.0, sq), axis=-1, keepdims=True)
        r = jnp.where(head_lo, lax.rsqrt(s_lo / HEAD_DIM + EPS), lax.rsqrt(s_hi / HEAD_DIM + EPS))
        n = (zj * r) * gqk_ref[:, j * LANES:(j + 1) * LANES]
        partner = jnp.where(half_lo, pltpu.roll(n, LANES - ROPE_HALF, 1), pltpu.roll(n, ROPE_HALF, 1))
        rot = n * cos + partner * sin
        if j < ATTN_WIDTH // LANES:
            qt_ref[j * LANES:(j + 1) * LANES, :] = (rot * (HEAD_DIM ** -0.5)).T.astype(BF16)
        else:
            k_ref[...] = rot.astype(BF16)
    vt_ref[...] = z[:, OFF_V:OFF_G].T.astype(BF16)


def _in_proj_call(x2, mod, g1, w1, gqk, cos_t, sin_t, w64, *, batch, seq, tm, shared_mod):
    n = batch * seq
    tps = seq // tm
    mod_idx = (lambda i: (0, 0, 0)) if shared_mod else (lambda i: (i // tps, 0, 0))
    const = lambda i: (0, 0)
    return pl.pallas_call(
        _in_proj_kernel,
        out_shape=(
            jax.ShapeDtypeStruct((n, CONV_WIDTH), BF16),
            jax.ShapeDtypeStruct((n, CONV_WIDTH), BF16),
            jax.ShapeDtypeStruct((n, 2 * FOURIER_WIDTH), BF16),
            jax.ShapeDtypeStruct((batch, ATTN_WIDTH, seq), BF16),
            jax.ShapeDtypeStruct((n, KV_WIDTH), BF16),
            jax.ShapeDtypeStruct((batch, KV_WIDTH, seq), BF16),
        ),
        grid=(n // tm,),
        in_specs=[
            pl.BlockSpec((tm, D_MODEL), lambda i: (i, 0)),
            pl.BlockSpec((None, N_MOD, D_MODEL), mod_idx),
            pl.BlockSpec((1, D_MODEL), const),
            pl.BlockSpec((D_MODEL, OFF_G), const),
            pl.BlockSpec((1, QK_WIDTH), const),
            pl.BlockSpec((tm, LANES), lambda i: (i % tps, 0)),
            pl.BlockSpec((tm, LANES), lambda i: (i % tps, 0)),
            pl.BlockSpec((FOURIER_WIDTH, 2 * FOURIER_WIDTH), const),
        ],
        out_specs=(
            pl.BlockSpec((tm, CONV_WIDTH), lambda i: (i, 0)),
            pl.BlockSpec((tm, CONV_WIDTH), lambda i: (i, 0)),
            pl.BlockSpec((tm, 2 * FOURIER_WIDTH), lambda i: (i, 0)),
            pl.BlockSpec((None, ATTN_WIDTH, tm), lambda i: (i // tps, 0, i % tps)),
            pl.BlockSpec((tm, KV_WIDTH), lambda i: (i, 0)),
            pl.BlockSpec((None, KV_WIDTH, tm), lambda i: (i // tps, 0, i % tps)),
        ),
        compiler_params=_params(("arbitrary",), 48),
        name="in_proj",
    )(x2, mod, g1, w1, gqk, cos_t, sin_t, w64)


def _fft_rows_kernel(z_ref, c_ref, s_ref, twc_ref, tws_ref, g_ref):
    zt = z_ref[...]
    p = _dot(c_ref[...], zt)
    q = _dot(s_ref[...], zt)
    nt = zt.shape[1] // (2 * FOURIER_WIDTH)
    w = FOURIER_WIDTH
    for t in range(nt):
        o = 2 * w * t
        gr = p[:, o:o + w] + q[:, o + w:o + 2 * w]
        gi = p[:, o + w:o + 2 * w] - q[:, o:o + w]
        tc = twc_ref[:, t:t + 1]
        ts = tws_ref[:, t:t + 1]
        g_ref[:, o:o + w] = (gr * tc + gi * ts).astype(BF16)
        g_ref[:, o + w:o + 2 * w] = (gi * tc - gr * ts).astype(BF16)


def _fft_rows_call(fab, c_r, s_r, twc, tws, *, batch, rows, nt):
    width = GRID_W * 2 * FOURIER_WIDTH
    blk = nt * 2 * FOURIER_WIDTH
    zf = fab.reshape(batch, rows, width)
    return pl.pallas_call(
        _fft_rows_kernel,
        out_shape=jax.ShapeDtypeStruct((batch, rows, width), BF16),
        grid=(batch, GRID_W // nt),
        in_specs=[
            pl.BlockSpec((None, rows, blk), lambda b, j: (b, 0, j)),
            pl.BlockSpec((rows, rows), lambda b, j: (0, 0)),
            pl.BlockSpec((rows, rows), lambda b, j: (0, 0)),
            pl.BlockSpec((None, rows, nt), lambda b, j: (j, 0, 0)),
            pl.BlockSpec((None, rows, nt), lambda b, j: (j, 0, 0)),
        ],
        out_specs=pl.BlockSpec((None, rows, blk), lambda b, j: (b, 0, j)),
        compiler_params=_params(("arbitrary", "arbitrary"), 32),
        name="fft_rows",
    )(zf, c_r, s_r, twc, tws)


def _fft_cols_kernel(g_ref, c_ref, s_ref, y_ref):
    w = FOURIER_WIDTH
    for t in range(g_ref.shape[0]):
        slab = g_ref[t]
        y = _dot(c_ref[...], slab[:, :w]) + _dot(s_ref[...], slab[:, w:])
        y_ref[:, t * w:(t + 1) * w] = y.astype(BF16)


def _fft_cols_call(g, c_c, s_c, *, batch, rows, kt):
    g4 = g.reshape(batch, rows, GRID_W, 2 * FOURIER_WIDTH)
    y = pl.pallas_call(
        _fft_cols_kernel,
        out_shape=jax.ShapeDtypeStruct((batch, GRID_W, rows * FOURIER_WIDTH), BF16),
        grid=(batch, rows // kt),
        in_specs=[
            pl.BlockSpec((None, kt, GRID_W, 2 * FOURIER_WIDTH), lambda b, j: (b, j, 0, 0)),
            pl.BlockSpec((GRID_W, GRID_W), lambda b, j: (0, 0)),
            pl.BlockSpec((GRID_W, GRID_W), lambda b, j: (0, 0)),
        ],
        out_specs=pl.BlockSpec((None, GRID_W, kt * FOURIER_WIDTH), lambda b, j: (b, 0, j)),
        compiler_params=_params(("arbitrary", "arbitrary"), 32),
        name="fft_cols",
    )(g4, c_c, s_c)
    return y.reshape(batch * GRID_W * rows, FOURIER_WIDTH)


def _dft_tables(n):
    idx = jnp.arange(n, dtype=jnp.int32)
    ang = (2.0 * jnp.pi / n) * ((idx[:, None] * idx[None, :]) % n).astype(F32)
    return jnp.cos(ang), jnp.sin(ang)


def _dft_dense_kernel(z_ref, c_ref, s_ref, y_ref):
    w = FOURIER_WIDTH
    y_ref[...] = (_dot(c_ref[...], z_ref[:, :w]) + _dot(s_ref[...], z_ref[:, w:])).astype(BF16)


def _dft_dense_call(fab, c_l, s_l, *, batch, seq):
    return pl.pallas_call(
        _dft_dense_kernel,
        out_shape=jax.ShapeDtypeStruct((batch * seq, FOURIER_WIDTH), BF16),
        grid=(batch,),
        in_specs=[
            pl.BlockSpec((seq, 2 * FOURIER_WIDTH), lambda b: (b, 0)),
            pl.BlockSpec((seq, seq), lambda b: (0, 0)),
            pl.BlockSpec((seq, seq), lambda b: (0, 0)),
        ],
        out_specs=pl.BlockSpec((seq, FOURIER_WIDTH), lambda b: (b, 0)),
        compiler_params=_params(("arbitrary",), 32),
        name="dft_dense",
    )(fab, c_l, s_l)


def _fourier_mix(fab, *, batch, seq):
    rows = seq // GRID_W
    scale = (seq * FOURIER_GROUP_DIM) ** -0.5
    if rows < BF16_SUBLANES:
        c_l, s_l = _dft_tables(seq)
        return _dft_dense_call(fab, (c_l * scale).astype(BF16), (s_l * scale).astype(BF16), batch=batch, seq=seq)
    c_r, s_r = _dft_tables(rows)
    c_c, s_c = _dft_tables(GRID_W)
    n2 = jnp.arange(GRID_W, dtype=jnp.int32)
    k1 = jnp.arange(rows, dtype=jnp.int32)
    tw_ang = (2.0 * jnp.pi / seq) * (k1[:, None] * n2[None, :]).astype(F32)
    nt = 8
    kt = min(rows, 8)
    split = lambda t: t.reshape(rows, GRID_W // nt, nt).transpose(1, 0, 2)
    g = _fft_rows_call(fab, c_r.astype(BF16), s_r.astype(BF16), split(jnp.cos(tw_ang)), split(jnp.sin(tw_ang)),
                       batch=batch, rows=rows, nt=nt)
    return _fft_cols_call(g, (c_c * scale).astype(BF16), (s_c * scale).astype(BF16),
                          batch=batch, rows=rows, kt=kt)


def _attn_kernel(*refs, chunks):
    n_src = len(chunks)
    qt_ref = refs[0]
    kv_refs = refs[1:1 + 2 * n_src]
    o_ref = refs[1 + 2 * n_src]
    g = pl.program_id(1)
    tq = qt_ref.shape[1]
    qt = qt_ref[...]
    row = lax.broadcasted_iota(jnp.int32, (KV_WIDTH, tq), 0)
    own_group = (row >= g * HEAD_DIM) & (row < (g + 1) * HEAD_DIM)

    def step(carry, kc, vc):
        m, l, acc = carry
        s = _dot(kc, qz)
        m_new = jnp.maximum(m, jnp.max(s, axis=0, keepdims=True))
        alpha = jnp.exp(m - m_new)
        p = jnp.exp(s - m_new)
        l_new = alpha * l + jnp.sum(p, axis=0, keepdims=True)
        acc_new = alpha * acc + _dot(vc, p.astype(BF16))
        return m_new, l_new, acc_new

    outs = []
    for h in range(GQA_GROUP):
        qh = qt[h * HEAD_DIM:(h + 1) * HEAD_DIM, :]
        qz = jnp.where(own_group, jnp.concatenate([qh] * N_KV_HEADS, axis=0), jnp.zeros((), BF16))
        carry = (jnp.full((1, tq), -jnp.inf, F32), jnp.zeros((1, tq), F32), jnp.zeros((HEAD_DIM, tq), F32))
        for src, (n_chunks, tk) in enumerate(chunks):
            k_ref, vt_ref = kv_refs[2 * src], kv_refs[2 * src + 1]
            if n_chunks == 1:
                carry = step(carry, k_ref[...], vt_ref[...])
            else:
                def body(c, carry, k_ref=k_ref, vt_ref=vt_ref, tk=tk):
                    start = pl.multiple_of(c * tk, tk)
                    return step(carry, k_ref[pl.ds(start, tk), :], vt_ref[:, pl.ds(start, tk)])
                carry = lax.fori_loop(0, n_chunks, body, carry)
        m, l, acc = carry
        outs.append(acc / l)
    o_ref[...] = jnp.concatenate(outs, axis=0).T.astype(BF16)


def _attn_call(qt, sources, *, batch, seq, tq, tk):
    nq = seq // tq
    gw = GQA_GROUP * HEAD_DIM
    in_specs = [pl.BlockSpec((None, gw, tq), lambda b, g, i: (b, g, i))]
    args = [qt]
    chunks = []
    for k, vt in sources:
        ln = k.shape[1]
        ck = min(tk, ln)
        chunks.append((ln // ck, ck))
        in_specs.append(pl.BlockSpec((None, ln, KV_WIDTH), lambda b, g, i: (b, 0, 0)))
        in_specs.append(pl.BlockSpec((None, HEAD_DIM, ln), lambda b, g, i: (b, g, 0)))
        args += [k, vt]
    return pl.pallas_call(
        functools.partial(_attn_kernel, chunks=tuple(chunks)),
        out_shape=jax.ShapeDtypeStruct((batch * seq, ATTN_WIDTH), BF16),
        grid=(batch, N_KV_HEADS, nq),
        in_specs=in_specs,
        out_specs=pl.BlockSpec((tq, gw), lambda b, g, i: (b * nq + i, g)),
        compiler_params=_params(("arbitrary", "arbitrary", "arbitrary"), 32),
        name="attention",
    )(*args)


def _merge_kernel(x_ref, mod_ref, g1_ref, wg_ref, bz_ref, u_ref, up_ref, un_ref, wconv_ref,
                  yf_ref, ya_ref, wc_ref, wf_ref, wa_ref, wo_ref, o_ref, *, tps):
    x = x_ref[...]
    tm = x.shape[0]
    h = _modulate(x, g1_ref[...], mod_ref[0:1, :], mod_ref[1:2, :]).astype(BF16)

    pos = pl.program_id(0) % tps
    u = u_ref[...].astype(F32)
    prev_row = jnp.where(pos == 0, 0.0, up_ref[...].astype(F32)[BF16_SUBLANES - 1:BF16_SUBLANES, :])
    next_row = jnp.where(pos == tps - 1, 0.0, un_ref[...].astype(F32)[0:1, :])
    row = lax.broadcasted_iota(jnp.int32, u.shape, 0)
    u_m1 = jnp.where(row == 0, prev_row, pltpu.roll(u, 1, 0))
    u_p1 = jnp.where(row == tm - 1, next_row, pltpu.roll(u, tm - 1, 0))
    conv = wconv_ref[0:1, :] * u_m1 + wconv_ref[1:2, :] * u + wconv_ref[2:3, :] * u_p1
    y_conv = (bz_ref[...].astype(F32) * conv).astype(BF16)

    d = D_MODEL
    m = _sigmoid(_dot(h, wg_ref[:, 0:d])) * _dot(y_conv, wc_ref[...])
    m = m + _sigmoid(_dot(h, wg_ref[:, d:2 * d])) * _dot(yf_ref[...], wf_ref[...])
    m = m + _sigmoid(_dot(h, wg_ref[:, 2 * d:3 * d])) * _dot(ya_ref[...], wa_ref[...])
    o_ref[...] = x + mod_ref[2:3, :] * _dot(m.astype(BF16), wo_ref[...])


def _merge_call(x2, mod, g1, wg, bz, u, wconv, yf, ya, wc, wf, wa, wo, *, batch, seq, tm, shared_mod):
    n = batch * seq
    tps = seq // tm
    hb = tm // BF16_SUBLANES
    n_hb = n // BF16_SUBLANES
    mod_idx = (lambda i: (0, 0, 0)) if shared_mod else (lambda i: (i // tps, 0, 0))
    const = lambda i: (0, 0)
    tile = lambda w: pl.BlockSpec((tm, w), lambda i: (i, 0))
    return pl.pallas_call(
        functools.partial(_merge_kernel, tps=tps),
        out_shape=jax.ShapeDtypeStruct((n, D_MODEL), F32),
        grid=(n // tm,),
        in_specs=[
            tile(D_MODEL),
            pl.BlockSpec((None, N_MOD, D_MODEL), mod_idx),
            pl.BlockSpec((1, D_MODEL), const),
            pl.BlockSpec((D_MODEL, N_BRANCHES * D_MODEL), const),
            tile(CONV_WIDTH),
            tile(CONV_WIDTH),
            pl.BlockSpec((BF16_SUBLANES, CONV_WIDTH), lambda i: (jnp.maximum(i * hb - 1, 0), 0)),
            pl.BlockSpec((BF16_SUBLANES, CONV_WIDTH), lambda i: (jnp.minimum((i + 1) * hb, n_hb - 1), 0)),
            pl.BlockSpec((CONV_K, CONV_WIDTH), const),
            tile(FOURIER_WIDTH),
            tile(ATTN_WIDTH),
            pl.BlockSpec((CONV_WIDTH, D_MODEL), const),
            pl.BlockSpec((FOURIER_WIDTH, D_MODEL), const),
            pl.BlockSpec((ATTN_WIDTH, D_MODEL), const),
            pl.BlockSpec((D_MODEL, D_MODEL), const),
        ],
        out_specs=tile(D_MODEL),
        compiler_params=_params(("arbitrary",), 52),
        name="merge",
    )(x2, mod, g1, wg, bz, u, u, u, wconv, yf, ya, wc, wf, wa, wo)


def _mlp_kernel(x_ref, mod_ref, g2_ref, w1_ref, w2_ref, o_ref, *, ff_chunk):
    x = x_ref[...]
    h = _modulate(x, g2_ref[...], mod_ref[3:4, :], mod_ref[4:5, :]).astype(BF16)
    acc = jnp.zeros(x.shape, F32)
    for c in range(D_FF // ff_chunk):
        a = jnp.maximum(_dot(h, w1_ref[:, c * ff_chunk:(c + 1) * ff_chunk]), 0.0)
        acc = acc + _dot((a * a).astype(BF16), w2_ref[c * ff_chunk:(c + 1) * ff_chunk, :])
    o_ref[...] = x + mod_ref[5:6, :] * acc


def _mlp_call(x2, mod, g2, w1, w2, *, batch, seq, tm, shared_mod):
    n = batch * seq
    tps = seq // tm
    mod_idx = (lambda i: (0, 0, 0)) if shared_mod else (lambda i: (i // tps, 0, 0))
    const = lambda i: (0, 0)
    return pl.pallas_call(
        functools.partial(_mlp_kernel, ff_chunk=1024),
        out_shape=jax.ShapeDtypeStruct((n, D_MODEL), F32),
        grid=(n // tm,),
        in_specs=[
            pl.BlockSpec((tm, D_MODEL), lambda i: (i, 0)),
            pl.BlockSpec((None, N_MOD, D_MODEL), mod_idx),
            pl.BlockSpec((1, D_MODEL), const),
            pl.BlockSpec((D_MODEL, D_FF), const),
            pl.BlockSpec((D_FF, D_MODEL), const),
        ],
        out_specs=pl.BlockSpec((tm, D_MODEL), lambda i: (i, 0)),
        compiler_params=_params(("arbitrary",), 56),
        name="mlp",
    )(x2, mod, g2, w1, w2)


def _rope_tables(rows):
    n_freq = ROPE_HALF // 2
    inv = ROPE_THETA ** (-jnp.arange(n_freq, dtype=F32) / n_freq)
    row_ang = jnp.repeat(jnp.arange(rows, dtype=F32)[:, None] * inv, GRID_W, axis=0)
    col_ang = jnp.tile(jnp.arange(GRID_W, dtype=F32)[:, None] * inv, (rows, 1))
    ang = jnp.concatenate([row_ang, col_ang], axis=-1)
    cos, sin = jnp.cos(ang), jnp.sin(ang)
    cos_h = jnp.concatenate([cos, cos], axis=-1)
    sin_h = jnp.concatenate([-sin, sin], axis=-1)
    reps = LANES // HEAD_DIM
    return jnp.tile(cos_h, (1, reps)), jnp.tile(sin_h, (1, reps))


def _channel_dft_matrix():
    c, s = _dft_tables(FOURIER_GROUP_DIM)
    eye = jnp.eye(FOURIER_GROUPS, dtype=F32)
    return jnp.concatenate([jnp.kron(eye, c), jnp.kron(eye, -s)], axis=1).astype(BF16)


def kernel(x, c, ctx, c_ctx, w_mod, b_mod, g_norm1, g_norm2, w_in, w_conv, g_q, g_k,
           w_conv_out, w_four_out, w_attn_out, w_o, w_ff1, w_ff2):
    batch, seq, d = x.shape
    ctx_len = ctx.shape[1]
    depth = w_mod.shape[0]
    rows = seq // GRID_W

    cc = jnp.zeros((MOD_ROWS, d), F32).at[:batch].set(c).at[batch].set(c_ctx)
    mods = _mod_call(cc, w_mod, b_mod)

    cos_t, sin_t = _rope_tables(rows)
    cos_c = jnp.ones((ctx_len, LANES), F32)
    sin_c = jnp.zeros((ctx_len, LANES), F32)
    w64 = _channel_dft_matrix()

    xs = x.reshape(batch * seq, d)
    cs = ctx.reshape(batch * ctx_len, d)
    for l in range(depth):
        mod_x = mods[l, :batch].reshape(batch, N_MOD, d)
        mod_c = mods[l, batch:batch + 1].reshape(1, N_MOD, d)
        g1 = g_norm1[l].reshape(1, d)
        g2 = g_norm2[l].reshape(1, d)
        w1 = w_in[l, :, :OFF_G].astype(BF16)
        wg = w_in[l, :, OFF_G:].astype(BF16)
        gqk = jnp.concatenate([jnp.tile(g_q[l], N_Q_HEADS), jnp.tile(g_k[l], N_KV_HEADS)]).reshape(1, QK_WIDTH)
        wc = w_conv_out[l].astype(BF16)
        wf = w_four_out[l].astype(BF16)
        wa = w_attn_out[l].astype(BF16)
        wo = w_o[l].astype(BF16)
        wf1 = w_ff1[l].astype(BF16)
        wf2 = w_ff2[l].astype(BF16)
        last = l == depth - 1

        bz_c, u_c, fab_c, qt_c, k_c, vt_c = _in_proj_call(
            cs, mod_c, g1, w1, gqk, cos_c, sin_c, w64, batch=batch, seq=ctx_len, tm=ctx_len, shared_mod=True)
        k_c3 = k_c.reshape(batch, ctx_len, KV_WIDTH)

        bz, u, fab, qt, k, vt = _in_proj_call(
            xs, mod_x, g1, w1, gqk, cos_t, sin_t, w64, batch=batch, seq=seq, tm=512, shared_mod=False)
        ya = _attn_call(qt, [(k.reshape(batch, seq, KV_WIDTH), vt), (k_c3, vt_c)],
                        batch=batch, seq=seq, tq=256, tk=512)
        yf = _fourier_mix(fab, batch=batch, seq=seq)
        xs = _merge_call(xs, mod_x, g1, wg, bz, u, w_conv[l], yf, ya, wc, wf, wa, wo,
                         batch=batch, seq=seq, tm=512, shared_mod=False)
        xs = _mlp_call(xs, mod_x, g2, wf1, wf2, batch=batch, seq=seq, tm=512, shared_mod=False)

        if not last:
            ya_c = _attn_call(qt_c, [(k_c3, vt_c)], batch=batch, seq=ctx_len, tq=ctx_len, tk=ctx_len)
            yf_c = _fourier_mix(fab_c, batch=batch, seq=ctx_len)
            cs = _merge_call(cs, mod_c, g1, wg, bz_c, u_c, w_conv[l], yf_c, ya_c, wc, wf, wa, wo,
                             batch=batch, seq=ctx_len, tm=ctx_len, shared_mod=True)
            cs = _mlp_call(cs, mod_c, g2, wf1, wf2, batch=batch, seq=ctx_len, tm=ctx_len, shared_mod=True)
    return xs.reshape(batch, seq, d)
```

```python
import functools

import jax
import jax.numpy as jnp
from jax import lax
from jax.experimental import pallas as pl
from jax.experimental.pallas import tpu as pltpu

D_MODEL = 1024
GRID_W = 64
HEAD_DIM = 64
N_Q_HEADS = 8
N_KV_HEADS = 2
GQA_GROUP = N_Q_HEADS // N_KV_HEADS
ATTN_WIDTH = N_Q_HEADS * HEAD_DIM
KV_WIDTH = N_KV_HEADS * HEAD_DIM
CONV_WIDTH = 256
CONV_K = 3
FOURIER_GROUPS = 4
FOURIER_GROUP_DIM = 64
FOURIER_WIDTH = FOURIER_GROUPS * FOURIER_GROUP_DIM
N_BRANCHES = 3
D_FF = 4 * D_MODEL
ROPE_THETA = 10000.0
ROPE_HALF = HEAD_DIM // 2
EPS = 1e-6
N_MOD = 6

OFF_B = 0
OFF_C = OFF_B + CONV_WIDTH
OFF_X = OFF_C + CONV_WIDTH
OFF_F = OFF_X + CONV_WIDTH
OFF_Q = OFF_F + FOURIER_WIDTH
OFF_K = OFF_Q + ATTN_WIDTH
OFF_V = OFF_K + KV_WIDTH
OFF_G = OFF_V + KV_WIDTH

LANES = 128
BF16_SUBLANES = 16
QK_WIDTH = ATTN_WIDTH + KV_WIDTH
V_ROWS = HEAD_DIM + BF16_SUBLANES
Q_SCALE = HEAD_DIM ** -0.5 * 1.4426950408889634
MOD_ROWS = 8

BF16 = jnp.bfloat16
F32 = jnp.float32


def _dot(a, b):
    return jnp.dot(a, b, preferred_element_type=F32)


def _params(semantics, vmem_mb):
    return pltpu.CompilerParams(dimension_semantics=semantics, vmem_limit_bytes=vmem_mb << 20)


def _modulate(x, g, shift, scale):
    ms = jnp.mean(x * x, axis=-1, keepdims=True)
    y = x * lax.rsqrt(ms + EPS)
    return (y * g) * (1.0 + scale) + shift


def _sigmoid(x):
    return 1.0 / (1.0 + jnp.exp(-x))


def _mod_kernel(c_ref, w_ref, b_ref, o_ref):
    c = c_ref[...]
    s = (c * _sigmoid(c)).astype(BF16)
    o_ref[...] = _dot(s, w_ref[...].astype(BF16)) + b_ref[...]


def _mod_call(cc, w_mod, b_mod):
    depth = w_mod.shape[0]
    tn = 1024
    return pl.pallas_call(
        _mod_kernel,
        out_shape=jax.ShapeDtypeStruct((depth, MOD_ROWS, N_MOD * D_MODEL), F32),
        grid=(depth, N_MOD * D_MODEL // tn),
        in_specs=[
            pl.BlockSpec((MOD_ROWS, D_MODEL), lambda l, j: (0, 0)),
            pl.BlockSpec((None, D_MODEL, tn), lambda l, j: (l, 0, j)),
            pl.BlockSpec((None, 1, tn), lambda l, j: (l, 0, j)),
        ],
        out_specs=pl.BlockSpec((None, MOD_ROWS, tn), lambda l, j: (l, 0, j)),
        compiler_params=_params(("arbitrary", "arbitrary"), 32),
        name="mod",
    )(cc, w_mod, b_mod.reshape(depth, 1, N_MOD * D_MODEL))


def _in_proj_kernel(x_ref, mod_ref, g1_ref, w_ref, gqk_ref, cos_ref, sin_ref, w64_ref,
                    bz_ref, u_ref, fab_ref, qt_ref, k_ref, vt_ref):
    h = _modulate(x_ref[...], g1_ref[...], mod_ref[0:1, :], mod_ref[1:2, :]).astype(BF16)
    z = _dot(h, w_ref[...])
    bz_ref[...] = z[:, OFF_B:OFF_C].astype(BF16)
    u_ref[...] = (z[:, OFF_C:OFF_X] * z[:, OFF_X:OFF_F]).astype(BF16)
    fab_ref[...] = _dot(z[:, OFF_F:OFF_Q].astype(BF16), w64_ref[...]).astype(BF16)

    tm = z.shape[0]
    lane = lax.broadcasted_iota(jnp.int32, (tm, LANES), 1)
    head_lo = lane < HEAD_DIM
    half_lo = (lane % HEAD_DIM) < ROPE_HALF
    cos = cos_ref[...]
    sin = sin_ref[...]
    for j in range(QK_WIDTH // LANES):
        zj = z[:, OFF_Q + j * LANES:OFF_Q + (j + 1) * LANES]
        sq = zj * zj
        s_lo = jnp.sum(jnp.where(head_lo, sq, 0.0), axis=-1, keepdims=True)
        s_hi = jnp.sum(jnp.where(head_lo, 0.0, sq), axis=-1, keepdims=True)
        r = jnp.where(head_lo, lax.rsqrt(s_lo / HEAD_DIM + EPS), lax.rsqrt(s_hi / HEAD_DIM + EPS))
        n = (zj * r) * gqk_ref[:, j * LANES:(j + 1) * LANES]
        partner = jnp.where(half_lo, pltpu.roll(n, LANES - ROPE_HALF, 1), pltpu.roll(n, ROPE_HALF, 1))
        rot = n * cos + partner * sin
        if j < ATTN_WIDTH // LANES:
            qt_ref[j * LANES:(j + 1) * LANES, :] = (rot * Q_SCALE).T.astype(BF16)
        else:
            k_ref[...] = rot.astype(BF16)
    vt = z[:, OFF_V:OFF_G].T.astype(BF16)
    for g in range(N_KV_HEADS):
        vt_ref[g * V_ROWS:g * V_ROWS + HEAD_DIM, :] = vt[g * HEAD_DIM:(g + 1) * HEAD_DIM, :]
        vt_ref[g * V_ROWS + HEAD_DIM:(g + 1) * V_ROWS, :] = jnp.ones((V_ROWS - HEAD_DIM, tm), BF16)


def _in_proj_call(x2, mod, g1, w1, gqk, cos_t, sin_t, w64, *, batch, seq, tm, shared_mod):
    n = batch * seq
    tps = seq // tm
    mod_idx = (lambda i: (0, 0, 0)) if shared_mod else (lambda i: (i // tps, 0, 0))
    const = lambda i: (0, 0)
    return pl.pallas_call(
        _in_proj_kernel,
        out_shape=(
            jax.ShapeDtypeStruct((n, CONV_WIDTH), BF16),
            jax.ShapeDtypeStruct((n, CONV_WIDTH), BF16),
            jax.ShapeDtypeStruct((n, 2 * FOURIER_WIDTH), BF16),
            jax.ShapeDtypeStruct((batch, ATTN_WIDTH, seq), BF16),
            jax.ShapeDtypeStruct((n, KV_WIDTH), BF16),
            jax.ShapeDtypeStruct((batch, N_KV_HEADS * V_ROWS, seq), BF16),
        ),
        grid=(n // tm,),
        in_specs=[
            pl.BlockSpec((tm, D_MODEL), lambda i: (i, 0)),
            pl.BlockSpec((None, N_MOD, D_MODEL), mod_idx),
            pl.BlockSpec((1, D_MODEL), const),
            pl.BlockSpec((D_MODEL, OFF_G), const),
            pl.BlockSpec((1, QK_WIDTH), const),
            pl.BlockSpec((tm, LANES), lambda i: (i % tps, 0)),
            pl.BlockSpec((tm, LANES), lambda i: (i % tps, 0)),
            pl.BlockSpec((FOURIER_WIDTH, 2 * FOURIER_WIDTH), const),
        ],
        out_specs=(
            pl.BlockSpec((tm, CONV_WIDTH), lambda i: (i, 0)),
            pl.BlockSpec((tm, CONV_WIDTH), lambda i: (i, 0)),
            pl.BlockSpec((tm, 2 * FOURIER_WIDTH), lambda i: (i, 0)),
            pl.BlockSpec((None, ATTN_WIDTH, tm), lambda i: (i // tps, 0, i % tps)),
            pl.BlockSpec((tm, KV_WIDTH), lambda i: (i, 0)),
            pl.BlockSpec((None, N_KV_HEADS * V_ROWS, tm), lambda i: (i // tps, 0, i % tps)),
        ),
        compiler_params=_params(("arbitrary",), 48),
        name="in_proj",
    )(x2, mod, g1, w1, gqk, cos_t, sin_t, w64)


def _fft_rows_kernel(z_ref, c_ref, s_ref, twc_ref, tws_ref, g_ref):
    zt = z_ref[...]
    p = _dot(c_ref[...], zt)
    q = _dot(s_ref[...], zt)
    nt = zt.shape[1] // (2 * FOURIER_WIDTH)
    w = FOURIER_WIDTH
    for t in range(nt):
        o = 2 * w * t
        gr = p[:, o:o + w] + q[:, o + w:o + 2 * w]
        gi = p[:, o + w:o + 2 * w] - q[:, o:o + w]
        tc = twc_ref[:, t:t + 1]
        ts = tws_ref[:, t:t + 1]
        g_ref[:, o:o + w] = (gr * tc + gi * ts).astype(BF16)
        g_ref[:, o + w:o + 2 * w] = (gi * tc - gr * ts).astype(BF16)


def _fft_rows_call(fab, c_r, s_r, twc, tws, *, batch, rows, nt):
    width = GRID_W * 2 * FOURIER_WIDTH
    blk = nt * 2 * FOURIER_WIDTH
    zf = fab.reshape(batch, rows, width)
    return pl.pallas_call(
        _fft_rows_kernel,
        out_shape=jax.ShapeDtypeStruct((batch, rows, width), BF16),
        grid=(batch, GRID_W // nt),
        in_specs=[
            pl.BlockSpec((None, rows, blk), lambda b, j: (b, 0, j)),
            pl.BlockSpec((rows, rows), lambda b, j: (0, 0)),
            pl.BlockSpec((rows, rows), lambda b, j: (0, 0)),
            pl.BlockSpec((None, rows, nt), lambda b, j: (j, 0, 0)),
            pl.BlockSpec((None, rows, nt), lambda b, j: (j, 0, 0)),
        ],
        out_specs=pl.BlockSpec((None, rows, blk), lambda b, j: (b, 0, j)),
        compiler_params=_params(("arbitrary", "arbitrary"), 32),
        name="fft_rows",
    )(zf, c_r, s_r, twc, tws)


def _fft_cols_kernel(g_ref, c_ref, s_ref, y_ref):
    w = FOURIER_WIDTH
    for t in range(g_ref.shape[0]):
        slab = g_ref[t]
        y = _dot(c_ref[...], slab[:, :w]) + _dot(s_ref[...], slab[:, w:])
        y_ref[:, t * w:(t + 1) * w] = y.astype(BF16)


def _fft_cols_call(g, c_c, s_c, *, batch, rows, kt):
    g4 = g.reshape(batch, rows, GRID_W, 2 * FOURIER_WIDTH)
    y = pl.pallas_call(
        _fft_cols_kernel,
        out_shape=jax.ShapeDtypeStruct((batch, GRID_W, rows * FOURIER_WIDTH), BF16),
        grid=(batch, rows // kt),
        in_specs=[
            pl.BlockSpec((None, kt, GRID_W, 2 * FOURIER_WIDTH), lambda b, j: (b, j, 0, 0)),
            pl.BlockSpec((GRID_W, GRID_W), lambda b, j: (0, 0)),
            pl.BlockSpec((GRID_W, GRID_W), lambda b, j: (0, 0)),
        ],
        out_specs=pl.BlockSpec((None, GRID_W, kt * FOURIER_WIDTH), lambda b, j: (b, 0, j)),
        compiler_params=_params(("arbitrary", "arbitrary"), 32),
        name="fft_cols",
    )(g4, c_c, s_c)
    return y.reshape(batch * GRID_W * rows, FOURIER_WIDTH)


def _dft_tables(n):
    idx = jnp.arange(n, dtype=jnp.int32)
    ang = (2.0 * jnp.pi / n) * ((idx[:, None] * idx[None, :]) % n).astype(F32)
    return jnp.cos(ang), jnp.sin(ang)


def _dft_dense_kernel(z_ref, c_ref, s_ref, y_ref):
    w = FOURIER_WIDTH
    y_ref[...] = (_dot(c_ref[...], z_ref[:, :w]) + _dot(s_ref[...], z_ref[:, w:])).astype(BF16)


def _dft_dense_call(fab, c_l, s_l, *, batch, seq):
    return pl.pallas_call(
        _dft_dense_kernel,
        out_shape=jax.ShapeDtypeStruct((batch * seq, FOURIER_WIDTH), BF16),
        grid=(batch,),
        in_specs=[
            pl.BlockSpec((seq, 2 * FOURIER_WIDTH), lambda b: (b, 0)),
            pl.BlockSpec((seq, seq), lambda b: (0, 0)),
            pl.BlockSpec((seq, seq), lambda b: (0, 0)),
        ],
        out_specs=pl.BlockSpec((seq, FOURIER_WIDTH), lambda b: (b, 0)),
        compiler_params=_params(("arbitrary",), 32),
        name="dft_dense",
    )(fab, c_l, s_l)


def _fourier_mix(fab, *, batch, seq):
    rows = seq // GRID_W
    scale = (seq * FOURIER_GROUP_DIM) ** -0.5
    if rows < BF16_SUBLANES:
        c_l, s_l = _dft_tables(seq)
        return _dft_dense_call(fab, (c_l * scale).astype(BF16), (s_l * scale).astype(BF16), batch=batch, seq=seq)
    c_r, s_r = _dft_tables(rows)
    c_c, s_c = _dft_tables(GRID_W)
    n2 = jnp.arange(GRID_W, dtype=jnp.int32)
    k1 = jnp.arange(rows, dtype=jnp.int32)
    tw_ang = (2.0 * jnp.pi / seq) * (k1[:, None] * n2[None, :]).astype(F32)
    nt = 8
    kt = min(rows, 8)
    split = lambda t: t.reshape(rows, GRID_W // nt, nt).transpose(1, 0, 2)
    g = _fft_rows_call(fab, c_r.astype(BF16), s_r.astype(BF16), split(jnp.cos(tw_ang)), split(jnp.sin(tw_ang)),
                       batch=batch, rows=rows, nt=nt)
    return _fft_cols_call(g, (c_c * scale).astype(BF16), (s_c * scale).astype(BF16),
                          batch=batch, rows=rows, kt=kt)


def _attn_kernel(*refs, chunks):
    n_src = len(chunks)
    qt_ref = refs[0]
    kv_refs = refs[1:1 + 2 * n_src]
    o_ref = refs[1 + 2 * n_src]
    s_scr = refs[2 + 2 * n_src]
    g = pl.program_id(1)
    tq = qt_ref.shape[1]
    qt = qt_ref[...]
    row = lax.broadcasted_iota(jnp.int32, (KV_WIDTH, tq), 0)
    own_group = (row >= g * HEAD_DIM) & (row < (g + 1) * HEAD_DIM)

    qz = [jnp.where(own_group, jnp.concatenate([qt[h * HEAD_DIM:(h + 1) * HEAD_DIM, :]] * N_KV_HEADS, axis=0),
                    jnp.zeros((), BF16)) for h in range(GQA_GROUP)]

    def scores(kc, slot):
        n = kc.shape[0]
        maxima = []
        for h in range(GQA_GROUP):
            s = _dot(kc, qz[h])
            s_scr[slot, h, 0:n, :] = s
            part = jnp.max(s.reshape(n // HEAD_DIM, HEAD_DIM, tq), axis=0)
            maxima.append(jnp.max(part, axis=0, keepdims=True))
        return tuple(maxima)

    def accumulate(carry, maxima, vc, slot):
        n = vc.shape[1]
        new = []
        for h in range(GQA_GROUP):
            m, acc = carry[h]
            m_new = jnp.maximum(m, maxima[h])
            p = jnp.exp2(s_scr[slot, h, 0:n, :] - m_new).astype(BF16)
            new.append((m_new, jnp.exp2(m - m_new) * acc + _dot(vc, p)))
        return tuple(new)

    def chunk_slice(src, c):
        tk = chunks[src][1]
        return pl.ds(c * tk if isinstance(c, int) else pl.multiple_of(c * tk, tk), tk)

    def k_chunk(src, c):
        return kv_refs[2 * src][chunk_slice(src, c), :]

    def v_chunk(src, c):
        return kv_refs[2 * src + 1][:, chunk_slice(src, c)]

    def pipe_step(state, slot, cur, nxt):
        carry, maxima = state
        nxt_maxima = scores(k_chunk(*nxt), 1 - slot) if nxt is not None else None
        return accumulate(carry, maxima, v_chunk(*cur), slot), nxt_maxima

    carry = tuple((jnp.full((1, tq), -jnp.inf, F32), jnp.zeros((V_ROWS, tq), F32)) for _ in range(GQA_GROUP))
    state = (carry, scores(k_chunk(0, 0), 0))
    slot = 0
    for src, (n_chunks, _) in enumerate(chunks):
        n_pairs = (n_chunks - 1) // 2
        if n_pairs >= 2:
            def pair(j, state, src=src, slot=slot):
                state = pipe_step(state, slot, (src, 2 * j), (src, 2 * j + 1))
                return pipe_step(state, 1 - slot, (src, 2 * j + 1), (src, 2 * j + 2))
            state = lax.fori_loop(0, n_pairs, pair, state)
            first = 2 * n_pairs
        else:
            first = 0
        for c in range(first, n_chunks):
            nxt = (src, c + 1) if c + 1 < n_chunks else ((src + 1, 0) if src + 1 < n_src else None)
            state = pipe_step(state, slot, (src, c), nxt)
            slot = 1 - slot
    outs = [acc[:HEAD_DIM, :] / acc[HEAD_DIM:HEAD_DIM + 1, :] for _, acc in state[0]]
    o_ref[...] = jnp.concatenate(outs, axis=0).T.astype(BF16)


def _attn_call(qt, sources, *, batch, seq, tq, tk):
    nq = seq // tq
    gw = GQA_GROUP * HEAD_DIM
    in_specs = [pl.BlockSpec((None, gw, tq), lambda b, g, i: (b, g, i))]
    args = [qt]
    chunks = []
    for k, vt in sources:
        ln = k.shape[1]
        ck = min(tk, ln)
        chunks.append((ln // ck, ck))
        in_specs.append(pl.BlockSpec((None, ln, KV_WIDTH), lambda b, g, i: (b, 0, 0)))
        in_specs.append(pl.BlockSpec((None, V_ROWS, ln), lambda b, g, i: (b, g, 0)))
        args += [k, vt]
    return pl.pallas_call(
        functools.partial(_attn_kernel, chunks=tuple(chunks)),
        out_shape=jax.ShapeDtypeStruct((batch * seq, ATTN_WIDTH), BF16),
        grid=(batch, N_KV_HEADS, nq),
        in_specs=in_specs,
        out_specs=pl.BlockSpec((tq, gw), lambda b, g, i: (b * nq + i, g)),
        scratch_shapes=[pltpu.VMEM((2, GQA_GROUP, max(ck for _, ck in chunks), tq), F32)],
        compiler_params=_params(("arbitrary", "arbitrary", "arbitrary"), 32),
        name="attention",
    )(*args)


def _merge_kernel(x_ref, mod_ref, g1_ref, wg_ref, bz_ref, u_ref, up_ref, un_ref, wconv_ref,
                  yf_ref, ya_ref, wc_ref, wf_ref, wa_ref, wo_ref, o_ref, *, tps):
    x = x_ref[...]
    tm = x.shape[0]
    h = _modulate(x, g1_ref[...], mod_ref[0:1, :], mod_ref[1:2, :]).astype(BF16)

    pos = pl.program_id(0) % tps
    u = u_ref[...].astype(F32)
    prev_row = jnp.where(pos == 0, 0.0, up_ref[...].astype(F32)[BF16_SUBLANES - 1:BF16_SUBLANES, :])
    next_row = jnp.where(pos == tps - 1, 0.0, un_ref[...].astype(F32)[0:1, :])
    row = lax.broadcasted_iota(jnp.int32, u.shape, 0)
    u_m1 = jnp.where(row == 0, prev_row, pltpu.roll(u, 1, 0))
    u_p1 = jnp.where(row == tm - 1, next_row, pltpu.roll(u, tm - 1, 0))
    conv = wconv_ref[0:1, :] * u_m1 + wconv_ref[1:2, :] * u + wconv_ref[2:3, :] * u_p1
    y_conv = (bz_ref[...].astype(F32) * conv).astype(BF16)

    d = D_MODEL
    m = _sigmoid(_dot(h, wg_ref[:, 0:d])) * _dot(y_conv, wc_ref[...])
    m = m + _sigmoid(_dot(h, wg_ref[:, d:2 * d])) * _dot(yf_ref[...], wf_ref[...])
    m = m + _sigmoid(_dot(h, wg_ref[:, 2 * d:3 * d])) * _dot(ya_ref[...], wa_ref[...])
    o_ref[...] = x + mod_ref[2:3, :] * _dot(m.astype(BF16), wo_ref[...])


def _merge_call(x2, mod, g1, wg, bz, u, wconv, yf, ya, wc, wf, wa, wo, *, batch, seq, tm, shared_mod):
    n = batch * seq
    tps = seq // tm
    hb = tm // BF16_SUBLANES
    n_hb = n // BF16_SUBLANES
    mod_idx = (lambda i: (0, 0, 0)) if shared_mod else (lambda i: (i // tps, 0, 0))
    const = lambda i: (0, 0)
    tile = lambda w: pl.BlockSpec((tm, w), lambda i: (i, 0))
    return pl.pallas_call(
        functools.partial(_merge_kernel, tps=tps),
        out_shape=jax.ShapeDtypeStruct((n, D_MODEL), F32),
        grid=(n // tm,),
        in_specs=[
            tile(D_MODEL),
            pl.BlockSpec((None, N_MOD, D_MODEL), mod_idx),
            pl.BlockSpec((1, D_MODEL), const),
            pl.BlockSpec((D_MODEL, N_BRANCHES * D_MODEL), const),
            tile(CONV_WIDTH),
            tile(CONV_WIDTH),
            pl.BlockSpec((BF16_SUBLANES, CONV_WIDTH), lambda i: (jnp.maximum(i * hb - 1, 0), 0)),
            pl.BlockSpec((BF16_SUBLANES, CONV_WIDTH), lambda i: (jnp.minimum((i + 1) * hb, n_hb - 1), 0)),
            pl.BlockSpec((CONV_K, CONV_WIDTH), const),
            tile(FOURIER_WIDTH),
            tile(ATTN_WIDTH),
            pl.BlockSpec((CONV_WIDTH, D_MODEL), const),
            pl.BlockSpec((FOURIER_WIDTH, D_MODEL), const),
            pl.BlockSpec((ATTN_WIDTH, D_MODEL), const),
            pl.BlockSpec((D_MODEL, D_MODEL), const),
        ],
        out_specs=tile(D_MODEL),
        compiler_params=_params(("arbitrary",), 52),
        name="merge",
    )(x2, mod, g1, wg, bz, u, u, u, wconv, yf, ya, wc, wf, wa, wo)


def _mlp_kernel(x_ref, mod_ref, g2_ref, w1_ref, w2_ref, o_ref, *, ff_chunk):
    x = x_ref[...]
    h = _modulate(x, g2_ref[...], mod_ref[3:4, :], mod_ref[4:5, :]).astype(BF16)
    acc = jnp.zeros(x.shape, F32)
    for c in range(D_FF // ff_chunk):
        a = jnp.maximum(_dot(h, w1_ref[:, c * ff_chunk:(c + 1) * ff_chunk]), 0.0)
        acc = acc + _dot((a * a).astype(BF16), w2_ref[c * ff_chunk:(c + 1) * ff_chunk, :])
    o_ref[...] = x + mod_ref[5:6, :] * acc


def _mlp_call(x2, mod, g2, w1, w2, *, batch, seq, tm, shared_mod):
    n = batch * seq
    tps = seq // tm
    mod_idx = (lambda i: (0, 0, 0)) if shared_mod else (lambda i: (i // tps, 0, 0))
    const = lambda i: (0, 0)
    return pl.pallas_call(
        functools.partial(_mlp_kernel, ff_chunk=1024),
        out_shape=jax.ShapeDtypeStruct((n, D_MODEL), F32),
        grid=(n // tm,),
        in_specs=[
            pl.BlockSpec((tm, D_MODEL), lambda i: (i, 0)),
            pl.BlockSpec((None, N_MOD, D_MODEL), mod_idx),
            pl.BlockSpec((1, D_MODEL), const),
            pl.BlockSpec((D_MODEL, D_FF), const),
            pl.BlockSpec((D_FF, D_MODEL), const),
        ],
        out_specs=pl.BlockSpec((tm, D_MODEL), lambda i: (i, 0)),
        compiler_params=_params(("arbitrary",), 56),
        name="mlp",
    )(x2, mod, g2, w1, w2)


def _rope_tables(rows):
    n_freq = ROPE_HALF // 2
    inv = ROPE_THETA ** (-jnp.arange(n_freq, dtype=F32) / n_freq)
    row_ang = jnp.repeat(jnp.arange(rows, dtype=F32)[:, None] * inv, GRID_W, axis=0)
    col_ang = jnp.tile(jnp.arange(GRID_W, dtype=F32)[:, None] * inv, (rows, 1))
    ang = jnp.concatenate([row_ang, col_ang], axis=-1)
    cos, sin = jnp.cos(ang), jnp.sin(ang)
    cos_h = jnp.concatenate([cos, cos], axis=-1)
    sin_h = jnp.concatenate([-sin, sin], axis=-1)
    reps = LANES // HEAD_DIM
    return jnp.tile(cos_h, (1, reps)), jnp.tile(sin_h, (1, reps))


def _channel_dft_matrix():
    c, s = _dft_tables(FOURIER_GROUP_DIM)
    eye = jnp.eye(FOURIER_GROUPS, dtype=F32)
    return jnp.concatenate([jnp.kron(eye, c), jnp.kron(eye, -s)], axis=1).astype(BF16)


def kernel(x, c, ctx, c_ctx, w_mod, b_mod, g_norm1, g_norm2, w_in, w_conv, g_q, g_k,
           w_conv_out, w_four_out, w_attn_out, w_o, w_ff1, w_ff2):
    batch, seq, d = x.shape
    ctx_len = ctx.shape[1]
    depth = w_mod.shape[0]
    rows = seq // GRID_W

    cc = jnp.zeros((MOD_ROWS, d), F32).at[:batch].set(c).at[batch].set(c_ctx)
    mods = _mod_call(cc, w_mod, b_mod)

    cos_t, sin_t = _rope_tables(rows)
    cos_c = jnp.ones((ctx_len, LANES), F32)
    sin_c = jnp.zeros((ctx_len, LANES), F32)
    w64 = _channel_dft_matrix()

    xs = x.reshape(batch * seq, d)
    cs = ctx.reshape(batch * ctx_len, d)
    for l in range(depth):
        mod_x = mods[l, :batch].reshape(batch, N_MOD, d)
        mod_c = mods[l, batch:batch + 1].reshape(1, N_MOD, d)
        g1 = g_norm1[l].reshape(1, d)
        g2 = g_norm2[l].reshape(1, d)
        w1 = w_in[l, :, :OFF_G].astype(BF16)
        wg = w_in[l, :, OFF_G:].astype(BF16)
        gqk = jnp.concatenate([jnp.tile(g_q[l], N_Q_HEADS), jnp.tile(g_k[l], N_KV_HEADS)]).reshape(1, QK_WIDTH)
        wc = w_conv_out[l].astype(BF16)
        wf = w_four_out[l].astype(BF16)
        wa = w_attn_out[l].astype(BF16)
        wo = w_o[l].astype(BF16)
        wf1 = w_ff1[l].astype(BF16)
        wf2 = w_ff2[l].astype(BF16)
        last = l == depth - 1

        bz_c, u_c, fab_c, qt_c, k_c, vt_c = _in_proj_call(
            cs, mod_c, g1, w1, gqk, cos_c, sin_c, w64, batch=batch, seq=ctx_len, tm=ctx_len, shared_mod=True)
        k_c3 = k_c.reshape(batch, ctx_len, KV_WIDTH)

        bz, u, fab, qt, k, vt = _in_proj_call(
            xs, mod_x, g1, w1, gqk, cos_t, sin_t, w64, batch=batch, seq=seq, tm=512, shared_mod=False)
        ya = _attn_call(qt, [(k.reshape(batch, seq, KV_WIDTH), vt), (k_c3, vt_c)],
                        batch=batch, seq=seq, tq=256, tk=512)
        yf = _fourier_mix(fab, batch=batch, seq=seq)
        xs = _merge_call(xs, mod_x, g1, wg, bz, u, w_conv[l], yf, ya, wc, wf, wa, wo,
                         batch=batch, seq=seq, tm=512, shared_mod=False)
        xs = _mlp_call(xs, mod_x, g2, wf1, wf2, batch=batch, seq=seq, tm=512, shared_mod=False)

        if not last:
            ya_c = _attn_call(qt_c, [(k_c3, vt_c)], batch=batch, seq=ctx_len, tq=ctx_len, tk=ctx_len)
            yf_c = _fourier_mix(fab_c, batch=batch, seq=ctx_len)
            cs = _merge_call(cs, mod_c, g1, wg, bz_c, u_c, w_conv[l], yf_c, ya_c, wc, wf, wa, wo,
                             batch=batch, seq=ctx_len, tm=ctx_len, shared_mod=True)
            cs = _mlp_call(cs, mod_c, g2, wf1, wf2, batch=batch, seq=ctx_len, tm=ctx_len, shared_mod=True)
    return xs.reshape(batch, seq, d)
```

```python
import functools

import jax
import jax.numpy as jnp
from jax import lax
from jax.experimental import pallas as pl
from jax.experimental.pallas import tpu as pltpu

D_MODEL = 1024
GRID_W = 64
HEAD_DIM = 64
N_Q_HEADS = 8
N_KV_HEADS = 2
GQA_GROUP = N_Q_HEADS // N_KV_HEADS
ATTN_WIDTH = N_Q_HEADS * HEAD_DIM
KV_WIDTH = N_KV_HEADS * HEAD_DIM
CONV_WIDTH = 256
CONV_K = 3
FOURIER_GROUPS = 4
FOURIER_GROUP_DIM = 64
FOURIER_WIDTH = FOURIER_GROUPS * FOURIER_GROUP_DIM
N_BRANCHES = 3
D_FF = 4 * D_MODEL
ROPE_THETA = 10000.0
ROPE_HALF = HEAD_DIM // 2
EPS = 1e-6
N_MOD = 6

OFF_B = 0
OFF_C = OFF_B + CONV_WIDTH
OFF_X = OFF_C + CONV_WIDTH
OFF_F = OFF_X + CONV_WIDTH
OFF_Q = OFF_F + FOURIER_WIDTH
OFF_K = OFF_Q + ATTN_WIDTH
OFF_V = OFF_K + KV_WIDTH
OFF_G = OFF_V + KV_WIDTH

LANES = 128
BF16_SUBLANES = 16
QK_WIDTH = ATTN_WIDTH + KV_WIDTH
V_ROWS = HEAD_DIM + BF16_SUBLANES
Q_SCALE = HEAD_DIM ** -0.5 * 1.4426950408889634
ROLLED_MIN_PAIRS = 8
IN_PROJ_SUB = 256
MERGE_SUB = 256
MOD_ROWS = 8

BF16 = jnp.bfloat16
F32 = jnp.float32


def _dot(a, b):
    return jnp.dot(a, b, preferred_element_type=F32)


def _params(semantics, vmem_mb):
    return pltpu.CompilerParams(dimension_semantics=semantics, vmem_limit_bytes=vmem_mb << 20)


def _modulate(x, g, shift, scale):
    ms = jnp.mean(x * x, axis=-1, keepdims=True)
    y = x * lax.rsqrt(ms + EPS)
    return (y * g) * (1.0 + scale) + shift


def _sigmoid(x):
    return 1.0 / (1.0 + jnp.exp(-x))


def _mod_kernel(c_ref, w_ref, b_ref, o_ref):
    c = c_ref[...]
    s = (c * _sigmoid(c)).astype(BF16)
    o_ref[...] = _dot(s, w_ref[...].astype(BF16)) + b_ref[...]


def _mod_call(cc, w_mod, b_mod):
    depth = w_mod.shape[0]
    tn = 1024
    return pl.pallas_call(
        _mod_kernel,
        out_shape=jax.ShapeDtypeStruct((depth, MOD_ROWS, N_MOD * D_MODEL), F32),
        grid=(depth, N_MOD * D_MODEL // tn),
        in_specs=[
            pl.BlockSpec((MOD_ROWS, D_MODEL), lambda l, j: (0, 0)),
            pl.BlockSpec((None, D_MODEL, tn), lambda l, j: (l, 0, j)),
            pl.BlockSpec((None, 1, tn), lambda l, j: (l, 0, j)),
        ],
        out_specs=pl.BlockSpec((None, MOD_ROWS, tn), lambda l, j: (l, 0, j)),
        compiler_params=_params(("arbitrary", "arbitrary"), 32),
        name="mod",
    )(cc, w_mod, b_mod.reshape(depth, 1, N_MOD * D_MODEL))


def _in_proj_kernel(x_ref, mod_ref, g1_ref, w_ref, gqk_ref, cos_ref, sin_ref, w64_ref,
                    bz_ref, u_ref, fab_ref, qt_ref, k_ref, vt_ref, *, sub):
    for r0 in range(0, x_ref.shape[0], sub):
        rows = slice(r0, r0 + sub)
        h = _modulate(x_ref[rows, :], g1_ref[...], mod_ref[0:1, :], mod_ref[1:2, :]).astype(BF16)
        z = _dot(h, w_ref[...])
        bz_ref[rows, :] = z[:, OFF_B:OFF_C].astype(BF16)
        u_ref[rows, :] = (z[:, OFF_C:OFF_X] * z[:, OFF_X:OFF_F]).astype(BF16)
        fab_ref[rows, :] = _dot(z[:, OFF_F:OFF_Q].astype(BF16), w64_ref[...]).astype(BF16)

        lane = lax.broadcasted_iota(jnp.int32, (sub, LANES), 1)
        head_lo = lane < HEAD_DIM
        half_lo = (lane % HEAD_DIM) < ROPE_HALF
        cos = cos_ref[rows, :]
        sin = sin_ref[rows, :]
        for j in range(QK_WIDTH // LANES):
            zj = z[:, OFF_Q + j * LANES:OFF_Q + (j + 1) * LANES]
            sq = zj * zj
            s_lo = jnp.sum(jnp.where(head_lo, sq, 0.0), axis=-1, keepdims=True)
            s_hi = jnp.sum(jnp.where(head_lo, 0.0, sq), axis=-1, keepdims=True)
            r = jnp.where(head_lo, lax.rsqrt(s_lo / HEAD_DIM + EPS), lax.rsqrt(s_hi / HEAD_DIM + EPS))
            n = (zj * r) * gqk_ref[:, j * LANES:(j + 1) * LANES]
            partner = jnp.where(half_lo, pltpu.roll(n, LANES - ROPE_HALF, 1), pltpu.roll(n, ROPE_HALF, 1))
            rot = n * cos + partner * sin
            if j < ATTN_WIDTH // LANES:
                qt_ref[j * LANES:(j + 1) * LANES, rows] = (rot * Q_SCALE).T.astype(BF16)
            else:
                k_ref[rows, :] = rot.astype(BF16)
        vt = z[:, OFF_V:OFF_G].T.astype(BF16)
        for g in range(N_KV_HEADS):
            vt_ref[g * V_ROWS:g * V_ROWS + HEAD_DIM, rows] = vt[g * HEAD_DIM:(g + 1) * HEAD_DIM, :]
            vt_ref[g * V_ROWS + HEAD_DIM:(g + 1) * V_ROWS, rows] = jnp.ones((V_ROWS - HEAD_DIM, sub), BF16)


def _in_proj_call(x2, mod, g1, w1, gqk, cos_t, sin_t, w64, *, batch, seq, tm, shared_mod):
    n = batch * seq
    tps = seq // tm
    mod_idx = (lambda i: (0, 0, 0)) if shared_mod else (lambda i: (i // tps, 0, 0))
    const = lambda i: (0, 0)
    return pl.pallas_call(
        functools.partial(_in_proj_kernel, sub=min(tm, IN_PROJ_SUB)),
        out_shape=(
            jax.ShapeDtypeStruct((n, CONV_WIDTH), BF16),
            jax.ShapeDtypeStruct((n, CONV_WIDTH), BF16),
            jax.ShapeDtypeStruct((n, 2 * FOURIER_WIDTH), BF16),
            jax.ShapeDtypeStruct((batch, ATTN_WIDTH, seq), BF16),
            jax.ShapeDtypeStruct((n, KV_WIDTH), BF16),
            jax.ShapeDtypeStruct((batch, N_KV_HEADS * V_ROWS, seq), BF16),
        ),
        grid=(n // tm,),
        in_specs=[
            pl.BlockSpec((tm, D_MODEL), lambda i: (i, 0)),
            pl.BlockSpec((None, N_MOD, D_MODEL), mod_idx),
            pl.BlockSpec((1, D_MODEL), const),
            pl.BlockSpec((D_MODEL, OFF_G), const),
            pl.BlockSpec((1, QK_WIDTH), const),
            pl.BlockSpec((tm, LANES), lambda i: (i % tps, 0)),
            pl.BlockSpec((tm, LANES), lambda i: (i % tps, 0)),
            pl.BlockSpec((FOURIER_WIDTH, 2 * FOURIER_WIDTH), const),
        ],
        out_specs=(
            pl.BlockSpec((tm, CONV_WIDTH), lambda i: (i, 0)),
            pl.BlockSpec((tm, CONV_WIDTH), lambda i: (i, 0)),
            pl.BlockSpec((tm, 2 * FOURIER_WIDTH), lambda i: (i, 0)),
            pl.BlockSpec((None, ATTN_WIDTH, tm), lambda i: (i // tps, 0, i % tps)),
            pl.BlockSpec((tm, KV_WIDTH), lambda i: (i, 0)),
            pl.BlockSpec((None, N_KV_HEADS * V_ROWS, tm), lambda i: (i // tps, 0, i % tps)),
        ),
        compiler_params=_params(("arbitrary",), 48),
        name="in_proj",
    )(x2, mod, g1, w1, gqk, cos_t, sin_t, w64)


def _fft_rows_kernel(z_ref, c_ref, s_ref, twc_ref, tws_ref, g_ref):
    zt = z_ref[...]
    p = _dot(c_ref[...], zt)
    q = _dot(s_ref[...], zt)
    nt = zt.shape[1] // (2 * FOURIER_WIDTH)
    w = FOURIER_WIDTH
    for t in range(nt):
        o = 2 * w * t
        gr = p[:, o:o + w] + q[:, o + w:o + 2 * w]
        gi = p[:, o + w:o + 2 * w] - q[:, o:o + w]
        tc = twc_ref[:, t:t + 1]
        ts = tws_ref[:, t:t + 1]
        g_ref[:, o:o + w] = (gr * tc + gi * ts).astype(BF16)
        g_ref[:, o + w:o + 2 * w] = (gi * tc - gr * ts).astype(BF16)


def _fft_rows_call(fab, c_r, s_r, twc, tws, *, batch, rows, nt):
    width = GRID_W * 2 * FOURIER_WIDTH
    blk = nt * 2 * FOURIER_WIDTH
    zf = fab.reshape(batch, rows, width)
    return pl.pallas_call(
        _fft_rows_kernel,
        out_shape=jax.ShapeDtypeStruct((batch, rows, width), BF16),
        grid=(batch, GRID_W // nt),
        in_specs=[
            pl.BlockSpec((None, rows, blk), lambda b, j: (b, 0, j)),
            pl.BlockSpec((rows, rows), lambda b, j: (0, 0)),
            pl.BlockSpec((rows, rows), lambda b, j: (0, 0)),
            pl.BlockSpec((None, rows, nt), lambda b, j: (j, 0, 0)),
            pl.BlockSpec((None, rows, nt), lambda b, j: (j, 0, 0)),
        ],
        out_specs=pl.BlockSpec((None, rows, blk), lambda b, j: (b, 0, j)),
        compiler_params=_params(("arbitrary", "arbitrary"), 32),
        name="fft_rows",
    )(zf, c_r, s_r, twc, tws)


def _fft_cols_kernel(g_ref, c_ref, s_ref, y_ref):
    w = FOURIER_WIDTH
    for t in range(g_ref.shape[0]):
        slab = g_ref[t]
        y = _dot(c_ref[...], slab[:, :w]) + _dot(s_ref[...], slab[:, w:])
        y_ref[:, t * w:(t + 1) * w] = y.astype(BF16)


def _fft_cols_call(g, c_c, s_c, *, batch, rows, kt):
    g4 = g.reshape(batch, rows, GRID_W, 2 * FOURIER_WIDTH)
    y = pl.pallas_call(
        _fft_cols_kernel,
        out_shape=jax.ShapeDtypeStruct((batch, GRID_W, rows * FOURIER_WIDTH), BF16),
        grid=(batch, rows // kt),
        in_specs=[
            pl.BlockSpec((None, kt, GRID_W, 2 * FOURIER_WIDTH), lambda b, j: (b, j, 0, 0)),
            pl.BlockSpec((GRID_W, GRID_W), lambda b, j: (0, 0)),
            pl.BlockSpec((GRID_W, GRID_W), lambda b, j: (0, 0)),
        ],
        out_specs=pl.BlockSpec((None, GRID_W, kt * FOURIER_WIDTH), lambda b, j: (b, 0, j)),
        compiler_params=_params(("arbitrary", "arbitrary"), 32),
        name="fft_cols",
    )(g4, c_c, s_c)
    return y.reshape(batch * GRID_W * rows, FOURIER_WIDTH)


def _dft_tables(n):
    idx = jnp.arange(n, dtype=jnp.int32)
    ang = (2.0 * jnp.pi / n) * ((idx[:, None] * idx[None, :]) % n).astype(F32)
    return jnp.cos(ang), jnp.sin(ang)


def _dft_dense_kernel(z_ref, c_ref, s_ref, y_ref):
    w = FOURIER_WIDTH
    y_ref[...] = (_dot(c_ref[...], z_ref[:, :w]) + _dot(s_ref[...], z_ref[:, w:])).astype(BF16)


def _dft_dense_call(fab, c_l, s_l, *, batch, seq):
    return pl.pallas_call(
        _dft_dense_kernel,
        out_shape=jax.ShapeDtypeStruct((batch * seq, FOURIER_WIDTH), BF16),
        grid=(batch,),
        in_specs=[
            pl.BlockSpec((seq, 2 * FOURIER_WIDTH), lambda b: (b, 0)),
            pl.BlockSpec((seq, seq), lambda b: (0, 0)),
            pl.BlockSpec((seq, seq), lambda b: (0, 0)),
        ],
        out_specs=pl.BlockSpec((seq, FOURIER_WIDTH), lambda b: (b, 0)),
        compiler_params=_params(("arbitrary",), 32),
        name="dft_dense",
    )(fab, c_l, s_l)


def _fourier_mix(fab, *, batch, seq):
    rows = seq // GRID_W
    scale = (seq * FOURIER_GROUP_DIM) ** -0.5
    if rows < BF16_SUBLANES:
        c_l, s_l = _dft_tables(seq)
        return _dft_dense_call(fab, (c_l * scale).astype(BF16), (s_l * scale).astype(BF16), batch=batch, seq=seq)
    c_r, s_r = _dft_tables(rows)
    c_c, s_c = _dft_tables(GRID_W)
    n2 = jnp.arange(GRID_W, dtype=jnp.int32)
    k1 = jnp.arange(rows, dtype=jnp.int32)
    tw_ang = (2.0 * jnp.pi / seq) * (k1[:, None] * n2[None, :]).astype(F32)
    nt = 8
    kt = min(rows, 8)
    split = lambda t: t.reshape(rows, GRID_W // nt, nt).transpose(1, 0, 2)
    g = _fft_rows_call(fab, c_r.astype(BF16), s_r.astype(BF16), split(jnp.cos(tw_ang)), split(jnp.sin(tw_ang)),
                       batch=batch, rows=rows, nt=nt)
    return _fft_cols_call(g, (c_c * scale).astype(BF16), (s_c * scale).astype(BF16),
                          batch=batch, rows=rows, kt=kt)


def _attn_kernel(*refs, chunks):
    n_src = len(chunks)
    qt_ref = refs[0]
    kv_refs = refs[1:1 + 2 * n_src]
    o_ref = refs[1 + 2 * n_src]
    s_scr = refs[2 + 2 * n_src]
    g = pl.program_id(1)
    tq = qt_ref.shape[1]
    qt = qt_ref[...]
    row = lax.broadcasted_iota(jnp.int32, (KV_WIDTH, tq), 0)
    own_group = (row >= g * HEAD_DIM) & (row < (g + 1) * HEAD_DIM)

    qz = [jnp.where(own_group, jnp.concatenate([qt[h * HEAD_DIM:(h + 1) * HEAD_DIM, :]] * N_KV_HEADS, axis=0),
                    jnp.zeros((), BF16)) for h in range(GQA_GROUP)]

    def scores(h, kc, slot):
        n = kc.shape[0]
        s = _dot(kc, qz[h])
        s_scr[slot, h, 0:n, :] = s
        part = jnp.max(s.reshape(n // HEAD_DIM, HEAD_DIM, tq), axis=0)
        return jnp.max(part, axis=0, keepdims=True)

    def accumulate(h, carry_h, chunk_max, vc, slot):
        n = vc.shape[1]
        m, acc = carry_h
        m_new = jnp.maximum(m, chunk_max)
        p = jnp.exp2(s_scr[slot, h, 0:n, :] - m_new).astype(BF16)
        return m_new, jnp.exp2(m - m_new) * acc + _dot(vc, p)

    def chunk_slice(src, c):
        tk = chunks[src][1]
        return pl.ds(c * tk if isinstance(c, int) else pl.multiple_of(c * tk, tk), tk)

    def k_chunk(src, c):
        return kv_refs[2 * src][chunk_slice(src, c), :]

    def v_chunk(src, c):
        return kv_refs[2 * src + 1][:, chunk_slice(src, c)]

    def pipe_step(state, slot, cur, nxt):
        carry, maxima = state
        vc = v_chunk(*cur)
        kc = k_chunk(*nxt) if nxt is not None else None
        new_carry, new_maxima = [], []
        for h in range(GQA_GROUP):
            if kc is not None:
                new_maxima.append(scores(h, kc, 1 - slot))
            new_carry.append(accumulate(h, carry[h], maxima[h], vc, slot))
        return tuple(new_carry), (tuple(new_maxima) if kc is not None else None)

    carry = tuple((jnp.full((1, tq), -jnp.inf, F32), jnp.zeros((V_ROWS, tq), F32)) for _ in range(GQA_GROUP))
    first_chunk = k_chunk(0, 0)
    state = (carry, tuple(scores(h, first_chunk, 0) for h in range(GQA_GROUP)))
    slot = 0
    for src, (n_chunks, _) in enumerate(chunks):
        n_pairs = (n_chunks - 1) // 2
        if n_pairs >= ROLLED_MIN_PAIRS:
            def pair(j, state, src=src, slot=slot):
                state = pipe_step(state, slot, (src, 2 * j), (src, 2 * j + 1))
                return pipe_step(state, 1 - slot, (src, 2 * j + 1), (src, 2 * j + 2))
            state = lax.fori_loop(0, n_pairs, pair, state)
            first = 2 * n_pairs
        else:
            first = 0
        for c in range(first, n_chunks):
            nxt = (src, c + 1) if c + 1 < n_chunks else ((src + 1, 0) if src + 1 < n_src else None)
            state = pipe_step(state, slot, (src, c), nxt)
            slot = 1 - slot
    outs = [acc[:HEAD_DIM, :] / acc[HEAD_DIM:HEAD_DIM + 1, :] for _, acc in state[0]]
    o_ref[...] = jnp.concatenate(outs, axis=0).T.astype(BF16)


def _attn_call(qt, sources, *, batch, seq, tq, tk):
    nq = seq // tq
    gw = GQA_GROUP * HEAD_DIM
    in_specs = [pl.BlockSpec((None, gw, tq), lambda b, g, i: (b, g, i))]
    args = [qt]
    chunks = []
    for k, vt in sources:
        ln = k.shape[1]
        ck = min(tk, ln)
        chunks.append((ln // ck, ck))
        in_specs.append(pl.BlockSpec((None, ln, KV_WIDTH), lambda b, g, i: (b, 0, 0)))
        in_specs.append(pl.BlockSpec((None, V_ROWS, ln), lambda b, g, i: (b, g, 0)))
        args += [k, vt]
    return pl.pallas_call(
        functools.partial(_attn_kernel, chunks=tuple(chunks)),
        out_shape=jax.ShapeDtypeStruct((batch * seq, ATTN_WIDTH), BF16),
        grid=(batch, N_KV_HEADS, nq),
        in_specs=in_specs,
        out_specs=pl.BlockSpec((tq, gw), lambda b, g, i: (b * nq + i, g)),
        scratch_shapes=[pltpu.VMEM((2, GQA_GROUP, max(ck for _, ck in chunks), tq), F32)],
        compiler_params=_params(("arbitrary", "arbitrary", "arbitrary"), 32),
        name="attention",
    )(*args)


def _merge_mlp_kernel(x_ref, mod_ref, g1_ref, g2_ref, wg_ref, bz_ref, u_ref, up_ref, un_ref, wconv_ref,
                      yf_ref, ya_ref, wc_ref, wf_ref, wa_ref, wo_ref, w1_ref, w2_ref, o_ref, *, tps, sub, ff_chunk):
    tm = x_ref.shape[0]

    pos = pl.program_id(0) % tps
    u = u_ref[...].astype(F32)
    prev_row = jnp.where(pos == 0, 0.0, up_ref[...].astype(F32)[BF16_SUBLANES - 1:BF16_SUBLANES, :])
    next_row = jnp.where(pos == tps - 1, 0.0, un_ref[...].astype(F32)[0:1, :])
    row = lax.broadcasted_iota(jnp.int32, u.shape, 0)
    u_m1 = jnp.where(row == 0, prev_row, pltpu.roll(u, 1, 0))
    u_p1 = jnp.where(row == tm - 1, next_row, pltpu.roll(u, tm - 1, 0))
    conv = wconv_ref[0:1, :] * u_m1 + wconv_ref[1:2, :] * u + wconv_ref[2:3, :] * u_p1
    y_conv = (bz_ref[...].astype(F32) * conv).astype(BF16)

    d = D_MODEL
    for r0 in range(0, tm, sub):
        rows = slice(r0, r0 + sub)
        x = x_ref[rows, :]
        h = _modulate(x, g1_ref[...], mod_ref[0:1, :], mod_ref[1:2, :]).astype(BF16)
        m = _sigmoid(_dot(h, wg_ref[:, 0:d])) * _dot(y_conv[rows, :], wc_ref[...])
        m = m + _sigmoid(_dot(h, wg_ref[:, d:2 * d])) * _dot(yf_ref[rows, :], wf_ref[...])
        m = m + _sigmoid(_dot(h, wg_ref[:, 2 * d:3 * d])) * _dot(ya_ref[rows, :], wa_ref[...])
        x1 = x + mod_ref[2:3, :] * _dot(m.astype(BF16), wo_ref[...])

        h2 = _modulate(x1, g2_ref[...], mod_ref[3:4, :], mod_ref[4:5, :]).astype(BF16)
        acc = jnp.zeros(x1.shape, F32)
        for c in range(D_FF // ff_chunk):
            a = jnp.maximum(_dot(h2, w1_ref[:, c * ff_chunk:(c + 1) * ff_chunk]), 0.0)
            acc = acc + _dot((a * a).astype(BF16), w2_ref[c * ff_chunk:(c + 1) * ff_chunk, :])
        o_ref[rows, :] = x1 + mod_ref[5:6, :] * acc


def _merge_mlp_call(x2, mod, g1, g2, wg, bz, u, wconv, yf, ya, wc, wf, wa, wo, w1, w2, *,
                    batch, seq, tm, shared_mod):
    n = batch * seq
    tps = seq // tm
    hb = tm // BF16_SUBLANES
    n_hb = n // BF16_SUBLANES
    mod_idx = (lambda i: (0, 0, 0)) if shared_mod else (lambda i: (i // tps, 0, 0))
    const = lambda i: (0, 0)
    tile = lambda w: pl.BlockSpec((tm, w), lambda i: (i, 0))
    weight = lambda r, c: pl.BlockSpec((r, c), const, pipeline_mode=pl.Buffered(1))
    return pl.pallas_call(
        functools.partial(_merge_mlp_kernel, tps=tps, sub=min(tm, MERGE_SUB), ff_chunk=1024),
        out_shape=jax.ShapeDtypeStruct((n, D_MODEL), F32),
        grid=(n // tm,),
        in_specs=[
            tile(D_MODEL),
            pl.BlockSpec((None, N_MOD, D_MODEL), mod_idx),
            pl.BlockSpec((1, D_MODEL), const),
            pl.BlockSpec((1, D_MODEL), const),
            weight(D_MODEL, N_BRANCHES * D_MODEL),
            tile(CONV_WIDTH),
            tile(CONV_WIDTH),
            pl.BlockSpec((BF16_SUBLANES, CONV_WIDTH), lambda i: (jnp.maximum(i * hb - 1, 0), 0)),
            pl.BlockSpec((BF16_SUBLANES, CONV_WIDTH), lambda i: (jnp.minimum((i + 1) * hb, n_hb - 1), 0)),
            pl.BlockSpec((CONV_K, CONV_WIDTH), const),
            tile(FOURIER_WIDTH),
            tile(ATTN_WIDTH),
            weight(CONV_WIDTH, D_MODEL),
            weight(FOURIER_WIDTH, D_MODEL),
            weight(ATTN_WIDTH, D_MODEL),
            weight(D_MODEL, D_MODEL),
            weight(D_MODEL, D_FF),
            weight(D_FF, D_MODEL),
        ],
        out_specs=tile(D_MODEL),
        compiler_params=_params(("arbitrary",), 56),
        name="merge_mlp",
    )(x2, mod, g1, g2, wg, bz, u, u, u, wconv, yf, ya, wc, wf, wa, wo, w1, w2)


def _rope_tables(rows):
    n_freq = ROPE_HALF // 2
    inv = ROPE_THETA ** (-jnp.arange(n_freq, dtype=F32) / n_freq)
    row_ang = jnp.repeat(jnp.arange(rows, dtype=F32)[:, None] * inv, GRID_W, axis=0)
    col_ang = jnp.tile(jnp.arange(GRID_W, dtype=F32)[:, None] * inv, (rows, 1))
    ang = jnp.concatenate([row_ang, col_ang], axis=-1)
    cos, sin = jnp.cos(ang), jnp.sin(ang)
    cos_h = jnp.concatenate([cos, cos], axis=-1)
    sin_h = jnp.concatenate([-sin, sin], axis=-1)
    reps = LANES // HEAD_DIM
    return jnp.tile(cos_h, (1, reps)), jnp.tile(sin_h, (1, reps))


def _channel_dft_matrix():
    c, s = _dft_tables(FOURIER_GROUP_DIM)
    eye = jnp.eye(FOURIER_GROUPS, dtype=F32)
    return jnp.concatenate([jnp.kron(eye, c), jnp.kron(eye, -s)], axis=1).astype(BF16)


def kernel(x, c, ctx, c_ctx, w_mod, b_mod, g_norm1, g_norm2, w_in, w_conv, g_q, g_k,
           w_conv_out, w_four_out, w_attn_out, w_o, w_ff1, w_ff2):
    batch, seq, d = x.shape
    ctx_len = ctx.shape[1]
    depth = w_mod.shape[0]
    rows = seq // GRID_W

    cc = jnp.zeros((MOD_ROWS, d), F32).at[:batch].set(c).at[batch].set(c_ctx)
    mods = _mod_call(cc, w_mod, b_mod)

    cos_t, sin_t = _rope_tables(rows)
    cos_c = jnp.ones((ctx_len, LANES), F32)
    sin_c = jnp.zeros((ctx_len, LANES), F32)
    w64 = _channel_dft_matrix()

    xs = x.reshape(batch * seq, d)
    cs = ctx.reshape(batch * ctx_len, d)
    for l in range(depth):
        mod_x = mods[l, :batch].reshape(batch, N_MOD, d)
        mod_c = mods[l, batch:batch + 1].reshape(1, N_MOD, d)
        g1 = g_norm1[l].reshape(1, d)
        g2 = g_norm2[l].reshape(1, d)
        w1 = w_in[l, :, :OFF_G].astype(BF16)
        wg = w_in[l, :, OFF_G:].astype(BF16)
        gqk = jnp.concatenate([jnp.tile(g_q[l], N_Q_HEADS), jnp.tile(g_k[l], N_KV_HEADS)]).reshape(1, QK_WIDTH)
        wc = w_conv_out[l].astype(BF16)
        wf = w_four_out[l].astype(BF16)
        wa = w_attn_out[l].astype(BF16)
        wo = w_o[l].astype(BF16)
        wf1 = w_ff1[l].astype(BF16)
        wf2 = w_ff2[l].astype(BF16)
        last = l == depth - 1

        bz_c, u_c, fab_c, qt_c, k_c, vt_c = _in_proj_call(
            cs, mod_c, g1, w1, gqk, cos_c, sin_c, w64, batch=batch, seq=ctx_len, tm=ctx_len, shared_mod=True)
        k_c3 = k_c.reshape(batch, ctx_len, KV_WIDTH)

        bz, u, fab, qt, k, vt = _in_proj_call(
            xs, mod_x, g1, w1, gqk, cos_t, sin_t, w64, batch=batch, seq=seq, tm=1024, shared_mod=False)
        ya = _attn_call(qt, [(k.reshape(batch, seq, KV_WIDTH), vt), (k_c3, vt_c)],
                        batch=batch, seq=seq, tq=256, tk=512)
        yf = _fourier_mix(fab, batch=batch, seq=seq)
        xs = _merge_mlp_call(xs, mod_x, g1, g2, wg, bz, u, w_conv[l], yf, ya, wc, wf, wa, wo, wf1, wf2,
                             batch=batch, seq=seq, tm=512, shared_mod=False)

        if not last:
            ya_c = _attn_call(qt_c, [(k_c3, vt_c)], batch=batch, seq=ctx_len, tq=ctx_len, tk=ctx_len)
            yf_c = _fourier_mix(fab_c, batch=batch, seq=ctx_len)
            cs = _merge_mlp_call(cs, mod_c, g1, g2, wg, bz_c, u_c, w_conv[l], yf_c, ya_c, wc, wf, wa, wo, wf1, wf2,
                                 batch=batch, seq=ctx_len, tm=ctx_len, shared_mod=True)
    return xs.reshape(batch, seq, d)
```

```python
import functools

import jax
import jax.numpy as jnp
from jax import lax
from jax.experimental import pallas as pl
from jax.experimental.pallas import tpu as pltpu

D_MODEL = 1024
GRID_W = 64
HEAD_DIM = 64
N_Q_HEADS = 8
N_KV_HEADS = 2
GQA_GROUP = N_Q_HEADS // N_KV_HEADS
ATTN_WIDTH = N_Q_HEADS * HEAD_DIM
KV_WIDTH = N_KV_HEADS * HEAD_DIM
CONV_WIDTH = 256
CONV_K = 3
FOURIER_GROUPS = 4
FOURIER_GROUP_DIM = 64
FOURIER_WIDTH = FOURIER_GROUPS * FOURIER_GROUP_DIM
N_BRANCHES = 3
D_FF = 4 * D_MODEL
ROPE_THETA = 10000.0
ROPE_HALF = HEAD_DIM // 2
EPS = 1e-6
N_MOD = 6

OFF_B = 0
OFF_C = OFF_B + CONV_WIDTH
OFF_X = OFF_C + CONV_WIDTH
OFF_F = OFF_X + CONV_WIDTH
OFF_Q = OFF_F + FOURIER_WIDTH
OFF_K = OFF_Q + ATTN_WIDTH
OFF_V = OFF_K + KV_WIDTH
OFF_G = OFF_V + KV_WIDTH

LANES = 128
SUBLANES = 8
BF16_SUBLANES = 16
QK_WIDTH = ATTN_WIDTH + KV_WIDTH
V_ROWS = HEAD_DIM + BF16_SUBLANES
Q_SCALE = HEAD_DIM ** -0.5 * 1.4426950408889634
ATTN_TQ = 256
ATTN_TK = 512
ATTN_Q_TILES = 2
CAST_STEPS = 8
IN_PROJ_TM = 1024
MERGE_TM = 512
IN_PROJ_SUB = 256
MERGE_SUB = 256
MOD_ROWS = 8

BF16 = jnp.bfloat16
F32 = jnp.float32


def _dot(a, b):
    return jnp.dot(a, b, preferred_element_type=F32)


def _params(semantics, vmem_mb):
    return pltpu.CompilerParams(dimension_semantics=semantics, vmem_limit_bytes=vmem_mb << 20)


def _modulate(x, g, shift, scale):
    ms = jnp.mean(x * x, axis=-1, keepdims=True)
    y = x * lax.rsqrt(ms + EPS)
    return (y * g) * (1.0 + scale) + shift


def _sigmoid(x):
    return 1.0 / (1.0 + jnp.exp(-x))


def _strided_pitch(n):
    tiles = -(-n // SUBLANES)
    return SUBLANES * (tiles if tiles % 2 else tiles + 1)


def _tiled_store(scr, rows, value):
    for j in range(scr.shape[0]):
        scr[j, rows, :] = value[:, j * LANES:(j + 1) * LANES]


def _tiled_load(scr, rows):
    return jnp.concatenate([scr[j, rows, :] for j in range(scr.shape[0])], axis=-1)


def _mod_kernel(c_ref, w_ref, b_ref, o_ref):
    c = c_ref[...]
    s = (c * _sigmoid(c)).astype(BF16)
    o_ref[...] = _dot(s, w_ref[...].astype(BF16)) + b_ref[...]


def _mod_call(cc, w_mod, b_mod):
    depth = w_mod.shape[0]
    tn = 1024
    return pl.pallas_call(
        _mod_kernel,
        out_shape=jax.ShapeDtypeStruct((depth, MOD_ROWS, N_MOD * D_MODEL), F32),
        grid=(depth, N_MOD * D_MODEL // tn),
        in_specs=[
            pl.BlockSpec((MOD_ROWS, D_MODEL), lambda l, j: (0, 0)),
            pl.BlockSpec((None, D_MODEL, tn), lambda l, j: (l, 0, j)),
            pl.BlockSpec((None, 1, tn), lambda l, j: (l, 0, j)),
        ],
        out_specs=pl.BlockSpec((None, MOD_ROWS, tn), lambda l, j: (l, 0, j)),
        compiler_params=_params(("arbitrary", "arbitrary"), 32),
        name="mod",
    )(cc, w_mod, b_mod.reshape(depth, 1, N_MOD * D_MODEL))


def _cast_kernel(win_ref, wco_ref, wfo_ref, wao_ref, wo_ref, wf1_ref, wf2_ref,
                 w1_out, wg_out, wco_out, wfo_out, wao_out, wo_out, wf1_out, wf2_out):
    w = win_ref[...]
    w1_out[...] = w[:, :OFF_G].astype(BF16)
    wg_out[...] = w[:, OFF_G:].astype(BF16)
    for src, dst in ((wco_ref, wco_out), (wfo_ref, wfo_out), (wao_ref, wao_out), (wo_ref, wo_out),
                     (wf1_ref, wf1_out), (wf2_ref, wf2_out)):
        dst[...] = src[...].astype(BF16)


def _cast_weights(w_in, w_conv_out, w_four_out, w_attn_out, w_o, w_ff1, w_ff2):
    depth = w_in.shape[0]
    ins = (w_in, w_conv_out, w_four_out, w_attn_out, w_o, w_ff1, w_ff2)
    out_cols = (OFF_G, w_in.shape[2] - OFF_G) + tuple(w.shape[2] for w in ins[1:])
    out_rows = (w_in.shape[1],) * 2 + tuple(w.shape[1] for w in ins[1:])
    spec = lambda r, c: pl.BlockSpec((None, r // CAST_STEPS, c), lambda l, i: (l, i, 0))
    return pl.pallas_call(
        _cast_kernel,
        out_shape=tuple(jax.ShapeDtypeStruct((depth, r, c), BF16) for r, c in zip(out_rows, out_cols)),
        grid=(depth, CAST_STEPS),
        in_specs=[spec(w.shape[1], w.shape[2]) for w in ins],
        out_specs=tuple(spec(r, c) for r, c in zip(out_rows, out_cols)),
        compiler_params=_params(("arbitrary", "arbitrary"), 40),
        name="cast_weights",
    )(*ins)


def _in_proj_kernel(x_ref, mod_ref, g1_ref, w_ref, gqk_ref, cos_ref, sin_ref, w64_ref,
                    bz_ref, u_ref, fab_ref, qt_ref, k_ref, vt_ref, *fab_scr, sub):
    tm = x_ref.shape[0]
    pitch = _strided_pitch(GRID_W)
    for r0 in range(0, tm, sub):
        rows = slice(r0, r0 + sub)
        h = _modulate(x_ref[rows, :], g1_ref[...], mod_ref[0:1, :], mod_ref[1:2, :]).astype(BF16)
        z = _dot(h, w_ref[...])
        bz_ref[rows, :] = z[:, OFF_B:OFF_C].astype(BF16)
        u_ref[rows, :] = (z[:, OFF_C:OFF_X] * z[:, OFF_X:OFF_F]).astype(BF16)
        fab = _dot(z[:, OFF_F:OFF_Q].astype(BF16), w64_ref[...])
        if fab_scr:
            for a in range(sub // GRID_W):
                start = (r0 // GRID_W + a) * pitch
                _tiled_store(fab_scr[0], slice(start, start + GRID_W), fab[a * GRID_W:(a + 1) * GRID_W, :])
        else:
            fab_ref[rows, :] = fab.astype(BF16)

        lane = lax.broadcasted_iota(jnp.int32, (sub, LANES), 1)
        head_lo = lane < HEAD_DIM
        half_lo = (lane % HEAD_DIM) < ROPE_HALF
        cos = cos_ref[rows, :]
        sin = sin_ref[rows, :]
        for j in range(QK_WIDTH // LANES):
            zj = z[:, OFF_Q + j * LANES:OFF_Q + (j + 1) * LANES]
            sq = zj * zj
            s_lo = jnp.sum(jnp.where(head_lo, sq, 0.0), axis=-1, keepdims=True)
            s_hi = jnp.sum(jnp.where(head_lo, 0.0, sq), axis=-1, keepdims=True)
            r = jnp.where(head_lo, lax.rsqrt(s_lo / HEAD_DIM + EPS), lax.rsqrt(s_hi / HEAD_DIM + EPS))
            n = (zj * r) * gqk_ref[:, j * LANES:(j + 1) * LANES]
            partner = jnp.where(half_lo, pltpu.roll(n, LANES - ROPE_HALF, 1), pltpu.roll(n, ROPE_HALF, 1))
            rot = n * cos + partner * sin
            if j < ATTN_WIDTH // LANES:
                qt_ref[j * LANES:(j + 1) * LANES, rows] = (rot * Q_SCALE).T.astype(BF16)
            else:
                k_ref[rows, :] = rot.astype(BF16)
        vt = z[:, OFF_V:OFF_G].T.astype(BF16)
        for g in range(N_KV_HEADS):
            vt_ref[g * V_ROWS:g * V_ROWS + HEAD_DIM, rows] = vt[g * HEAD_DIM:(g + 1) * HEAD_DIM, :]
            vt_ref[g * V_ROWS + HEAD_DIM:(g + 1) * V_ROWS, rows] = jnp.ones((V_ROWS - HEAD_DIM, sub), BF16)
    if fab_scr:
        for n2 in range(GRID_W):
            fab_ref[n2] = _tiled_load(fab_scr[0], pl.ds(n2, tm // GRID_W, stride=pitch)).astype(BF16)


def _in_proj_call(x2, mod, g1, w1, gqk, cos_t, sin_t, w64, *, layer, batch, seq, tm, shared_mod):
    n = batch * seq
    tps = seq // tm
    mod_idx = (lambda i: (0, 0, 0)) if shared_mod else (lambda i: (i // tps, 0, 0))
    const = lambda i: (0, 0)
    tile_rows = tm // GRID_W
    column_major = tile_rows >= BF16_SUBLANES
    if column_major:
        fab_shape = jax.ShapeDtypeStruct((batch, GRID_W, seq // GRID_W, 2 * FOURIER_WIDTH), BF16)
        fab_spec = pl.BlockSpec((None, GRID_W, tile_rows, 2 * FOURIER_WIDTH), lambda i: (i // tps, 0, i % tps, 0))
        scratch = [pltpu.VMEM((2 * FOURIER_WIDTH // LANES, tile_rows * _strided_pitch(GRID_W), LANES), F32)]
    else:
        fab_shape = jax.ShapeDtypeStruct((n, 2 * FOURIER_WIDTH), BF16)
        fab_spec = pl.BlockSpec((tm, 2 * FOURIER_WIDTH), lambda i: (i, 0))
        scratch = []
    return pl.pallas_call(
        functools.partial(_in_proj_kernel, sub=min(tm, IN_PROJ_SUB)),
        out_shape=(
            jax.ShapeDtypeStruct((n, CONV_WIDTH), BF16),
            jax.ShapeDtypeStruct((n, CONV_WIDTH), BF16),
            fab_shape,
            jax.ShapeDtypeStruct((batch, ATTN_WIDTH, seq), BF16),
            jax.ShapeDtypeStruct((n, KV_WIDTH), BF16),
            jax.ShapeDtypeStruct((batch, N_KV_HEADS * V_ROWS, seq), BF16),
        ),
        grid=(n // tm,),
        in_specs=[
            pl.BlockSpec((tm, D_MODEL), lambda i: (i, 0)),
            pl.BlockSpec((None, N_MOD, D_MODEL), mod_idx),
            pl.BlockSpec((1, D_MODEL), const),
            pl.BlockSpec((None, D_MODEL, OFF_G), lambda i: (layer, 0, 0)),
            pl.BlockSpec((1, QK_WIDTH), const),
            pl.BlockSpec((tm, LANES), lambda i: (i % tps, 0)),
            pl.BlockSpec((tm, LANES), lambda i: (i % tps, 0)),
            pl.BlockSpec((FOURIER_WIDTH, 2 * FOURIER_WIDTH), const),
        ],
        out_specs=(
            pl.BlockSpec((tm, CONV_WIDTH), lambda i: (i, 0)),
            pl.BlockSpec((tm, CONV_WIDTH), lambda i: (i, 0)),
            fab_spec,
            pl.BlockSpec((None, ATTN_WIDTH, tm), lambda i: (i // tps, 0, i % tps)),
            pl.BlockSpec((tm, KV_WIDTH), lambda i: (i, 0)),
            pl.BlockSpec((None, N_KV_HEADS * V_ROWS, tm), lambda i: (i // tps, 0, i % tps)),
        ),
        scratch_shapes=scratch,
        compiler_params=_params(("arbitrary",), 48),
        name="in_proj",
    )(x2, mod, g1, w1, gqk, cos_t, sin_t, w64)


def _fft_kernel(z_ref, cr_ref, sr_ref, twc_ref, tws_ref, cc_ref, sc_ref, y_ref, g_scr, y_scr):
    w = FOURIER_WIDTH
    rows = z_ref.shape[1]
    pitch = _strided_pitch(rows)
    for n2 in range(GRID_W):
        zb = z_ref[n2]
        p = _dot(cr_ref[...], zb)
        q = _dot(sr_ref[...], zb)
        gr = p[:, :w] + q[:, w:]
        gi = p[:, w:] - q[:, :w]
        tc = twc_ref[n2]
        ts = tws_ref[n2]
        g = jnp.concatenate([gr * tc + gi * ts, gi * tc - gr * ts], axis=-1)
        _tiled_store(g_scr, slice(n2 * pitch, n2 * pitch + rows), g)
    for k1 in range(rows):
        slab = _tiled_load(g_scr, pl.ds(k1, GRID_W, stride=pitch)).astype(BF16)
        y = _dot(cc_ref[...], slab[:, :w]) + _dot(sc_ref[...], slab[:, w:])
        _tiled_store(y_scr, pl.ds(k1, GRID_W, stride=pitch), y)
    for k2 in range(GRID_W):
        y_ref[k2 * rows:(k2 + 1) * rows, :] = _tiled_load(y_scr, slice(k2 * pitch, k2 * pitch + rows)).astype(BF16)


def _fft_call(zt, c_r, s_r, twc, tws, c_c, s_c, *, batch, rows):
    seq = rows * GRID_W
    const2 = lambda b: (0, 0)
    const3 = lambda b: (0, 0, 0)
    return pl.pallas_call(
        _fft_kernel,
        out_shape=jax.ShapeDtypeStruct((batch * seq, FOURIER_WIDTH), BF16),
        grid=(batch,),
        in_specs=[
            pl.BlockSpec((None, GRID_W, rows, 2 * FOURIER_WIDTH), lambda b: (b, 0, 0, 0)),
            pl.BlockSpec((rows, rows), const2),
            pl.BlockSpec((rows, rows), const2),
            pl.BlockSpec((GRID_W, rows, 1), const3),
            pl.BlockSpec((GRID_W, rows, 1), const3),
            pl.BlockSpec((GRID_W, GRID_W), const2),
            pl.BlockSpec((GRID_W, GRID_W), const2),
        ],
        out_specs=pl.BlockSpec((seq, FOURIER_WIDTH), lambda b: (b, 0)),
        scratch_shapes=[pltpu.VMEM((2 * FOURIER_WIDTH // LANES, GRID_W * _strided_pitch(rows), LANES), F32),
                        pltpu.VMEM((FOURIER_WIDTH // LANES, GRID_W * _strided_pitch(rows), LANES), F32)],
        compiler_params=_params(("arbitrary",), 40),
        name="fft",
    )(zt, c_r, s_r, twc, tws, c_c, s_c)


def _dft_tables(n):
    idx = jnp.arange(n, dtype=jnp.int32)
    ang = (2.0 * jnp.pi / n) * ((idx[:, None] * idx[None, :]) % n).astype(F32)
    return jnp.cos(ang), jnp.sin(ang)


def _dft_dense_kernel(z_ref, c_ref, s_ref, y_ref):
    w = FOURIER_WIDTH
    y_ref[...] = (_dot(c_ref[...], z_ref[:, :w]) + _dot(s_ref[...], z_ref[:, w:])).astype(BF16)


def _dft_dense_call(fab, c_l, s_l, *, batch, seq):
    return pl.pallas_call(
        _dft_dense_kernel,
        out_shape=jax.ShapeDtypeStruct((batch * seq, FOURIER_WIDTH), BF16),
        grid=(batch,),
        in_specs=[
            pl.BlockSpec((seq, 2 * FOURIER_WIDTH), lambda b: (b, 0)),
            pl.BlockSpec((seq, seq), lambda b: (0, 0)),
            pl.BlockSpec((seq, seq), lambda b: (0, 0)),
        ],
        out_specs=pl.BlockSpec((seq, FOURIER_WIDTH), lambda b: (b, 0)),
        compiler_params=_params(("arbitrary",), 32),
        name="dft_dense",
    )(fab, c_l, s_l)


def _fourier_mix(fab, *, batch, seq):
    rows = seq // GRID_W
    scale = (seq * FOURIER_GROUP_DIM) ** -0.5
    if rows < BF16_SUBLANES:
        c_l, s_l = _dft_tables(seq)
        return _dft_dense_call(fab, (c_l * scale).astype(BF16), (s_l * scale).astype(BF16), batch=batch, seq=seq)
    c_r, s_r = _dft_tables(rows)
    c_c, s_c = _dft_tables(GRID_W)
    n2 = jnp.arange(GRID_W, dtype=jnp.int32)
    k1 = jnp.arange(rows, dtype=jnp.int32)
    tw_ang = (2.0 * jnp.pi / seq) * (n2[:, None] * k1[None, :]).astype(F32)[:, :, None]
    return _fft_call(fab, c_r.astype(BF16), s_r.astype(BF16), jnp.cos(tw_ang), jnp.sin(tw_ang),
                     (c_c * scale).astype(BF16), (s_c * scale).astype(BF16), batch=batch, rows=rows)


def _attn_kernel(*refs, chunks, tq):
    n_src = len(chunks)
    qt_ref = refs[0]
    kv_refs = refs[1:1 + 2 * n_src]
    o_ref = refs[1 + 2 * n_src]
    s_scr = refs[2 + 2 * n_src]
    g = pl.program_id(1)
    row = lax.broadcasted_iota(jnp.int32, (KV_WIDTH, tq), 0)
    own_group = (row >= g * HEAD_DIM) & (row < (g + 1) * HEAD_DIM)
    steps = [(src, c) for src, (n_chunks, _) in enumerate(chunks) for c in range(n_chunks)]

    def k_chunk(src, c):
        tk = chunks[src][1]
        return kv_refs[2 * src][c * tk:(c + 1) * tk, :]

    def v_chunk(src, c):
        tk = chunks[src][1]
        return kv_refs[2 * src + 1][:, c * tk:(c + 1) * tk]

    def scores(qz_h, h, kc, slot):
        n = kc.shape[0]
        s = _dot(kc, qz_h)
        s_scr[slot, h, 0:n, :] = s
        part = jnp.max(s.reshape(n // HEAD_DIM, HEAD_DIM, tq), axis=0)
        return jnp.max(part, axis=0, keepdims=True)

    def accumulate(h, carry_h, chunk_max, vc, slot):
        n = vc.shape[1]
        m, acc = carry_h
        m_new = jnp.maximum(m, chunk_max)
        p = jnp.exp2(s_scr[slot, h, 0:n, :] - m_new).astype(BF16)
        return m_new, jnp.exp2(m - m_new) * acc + _dot(vc, p)

    slot = 0
    for q0 in range(0, qt_ref.shape[1], tq):
        qt = qt_ref[:, q0:q0 + tq]
        qz = [jnp.where(own_group, jnp.concatenate([qt[h * HEAD_DIM:(h + 1) * HEAD_DIM, :]] * N_KV_HEADS, axis=0),
                        jnp.zeros((), BF16)) for h in range(GQA_GROUP)]
        carry = [(jnp.full((1, tq), -jnp.inf, F32), jnp.zeros((V_ROWS, tq), F32)) for _ in range(GQA_GROUP)]
        first_chunk = k_chunk(*steps[0])
        maxima = [scores(qz[h], h, first_chunk, slot) for h in range(GQA_GROUP)]
        for t, cur in enumerate(steps):
            vc = v_chunk(*cur)
            kc = k_chunk(*steps[t + 1]) if t + 1 < len(steps) else None
            next_maxima = []
            for h in range(GQA_GROUP):
                if kc is not None:
                    next_maxima.append(scores(qz[h], h, kc, 1 - slot))
                carry[h] = accumulate(h, carry[h], maxima[h], vc, slot)
            maxima = next_maxima
            slot = 1 - slot
        outs = [acc[:HEAD_DIM, :] / acc[HEAD_DIM:HEAD_DIM + 1, :] for _, acc in carry]
        o_ref[q0:q0 + tq, :] = jnp.concatenate(outs, axis=0).T.astype(BF16)


def _attn_call(qt, sources, *, batch, seq, tq, tk, q_tiles):
    bq = tq * q_tiles
    nq = seq // bq
    gw = GQA_GROUP * HEAD_DIM
    in_specs = [pl.BlockSpec((None, gw, bq), lambda b, g, i: (b, g, i))]
    args = [qt]
    chunks = []
    for k, vt in sources:
        ln = k.shape[1]
        ck = min(tk, ln)
        chunks.append((ln // ck, ck))
        in_specs.append(pl.BlockSpec((None, ln, KV_WIDTH), lambda b, g, i: (b, 0, 0)))
        in_specs.append(pl.BlockSpec((None, V_ROWS, ln), lambda b, g, i: (b, g, 0)))
        args += [k, vt]
    return pl.pallas_call(
        functools.partial(_attn_kernel, chunks=tuple(chunks), tq=tq),
        out_shape=jax.ShapeDtypeStruct((batch * seq, ATTN_WIDTH), BF16),
        grid=(batch, N_KV_HEADS, nq),
        in_specs=in_specs,
        out_specs=pl.BlockSpec((bq, gw), lambda b, g, i: (b * nq + i, g)),
        scratch_shapes=[pltpu.VMEM((2, GQA_GROUP, max(ck for _, ck in chunks), tq), F32)],
        compiler_params=_params(("arbitrary", "arbitrary", "arbitrary"), 32),
        name="attention",
    )(*args)


def _merge_mlp_kernel(x_ref, mod_ref, g1_ref, g2_ref, wg_ref, bz_ref, u_ref, up_ref, un_ref, wconv_ref,
                      yf_ref, ya_ref, wc_ref, wf_ref, wa_ref, wo_ref, w1_ref, w2_ref, o_ref, *, tps, sub, ff_chunk):
    tm = x_ref.shape[0]

    pos = pl.program_id(0) % tps
    u = u_ref[...].astype(F32)
    prev_row = jnp.where(pos == 0, 0.0, up_ref[...].astype(F32)[BF16_SUBLANES - 1:BF16_SUBLANES, :])
    next_row = jnp.where(pos == tps - 1, 0.0, un_ref[...].astype(F32)[0:1, :])
    row = lax.broadcasted_iota(jnp.int32, u.shape, 0)
    u_m1 = jnp.where(row == 0, prev_row, pltpu.roll(u, 1, 0))
    u_p1 = jnp.where(row == tm - 1, next_row, pltpu.roll(u, tm - 1, 0))
    conv = wconv_ref[0:1, :] * u_m1 + wconv_ref[1:2, :] * u + wconv_ref[2:3, :] * u_p1
    y_conv = (bz_ref[...].astype(F32) * conv).astype(BF16)

    d = D_MODEL
    for r0 in range(0, tm, sub):
        rows = slice(r0, r0 + sub)
        x = x_ref[rows, :]
        h = _modulate(x, g1_ref[...], mod_ref[0:1, :], mod_ref[1:2, :]).astype(BF16)
        m = _sigmoid(_dot(h, wg_ref[:, 0:d])) * _dot(y_conv[rows, :], wc_ref[...])
        m = m + _sigmoid(_dot(h, wg_ref[:, d:2 * d])) * _dot(yf_ref[rows, :], wf_ref[...])
        m = m + _sigmoid(_dot(h, wg_ref[:, 2 * d:3 * d])) * _dot(ya_ref[rows, :], wa_ref[...])
        x1 = x + mod_ref[2:3, :] * _dot(m.astype(BF16), wo_ref[...])

        h2 = _modulate(x1, g2_ref[...], mod_ref[3:4, :], mod_ref[4:5, :]).astype(BF16)
        acc = jnp.zeros(x1.shape, F32)
        for c in range(D_FF // ff_chunk):
            a = jnp.maximum(_dot(h2, w1_ref[:, c * ff_chunk:(c + 1) * ff_chunk]), 0.0)
            acc = acc + _dot((a * a).astype(BF16), w2_ref[c * ff_chunk:(c + 1) * ff_chunk, :])
        o_ref[rows, :] = x1 + mod_ref[5:6, :] * acc


def _merge_mlp_call(x2, mod, g1, g2, wg, bz, u, wconv, yf, ya, wc, wf, wa, wo, w1, w2, *,
                    layer, batch, seq, tm, shared_mod):
    n = batch * seq
    tps = seq // tm
    hb = tm // BF16_SUBLANES
    n_hb = n // BF16_SUBLANES
    mod_idx = (lambda i: (0, 0, 0)) if shared_mod else (lambda i: (i // tps, 0, 0))
    const = lambda i: (0, 0)
    tile = lambda w: pl.BlockSpec((tm, w), lambda i: (i, 0))
    weight = lambda r, c: pl.BlockSpec((None, r, c), lambda i: (layer, 0, 0), pipeline_mode=pl.Buffered(1))
    return pl.pallas_call(
        functools.partial(_merge_mlp_kernel, tps=tps, sub=min(tm, MERGE_SUB), ff_chunk=1024),
        out_shape=jax.ShapeDtypeStruct((n, D_MODEL), F32),
        grid=(n // tm,),
        in_specs=[
            tile(D_MODEL),
            pl.BlockSpec((None, N_MOD, D_MODEL), mod_idx),
            pl.BlockSpec((1, D_MODEL), const),
            pl.BlockSpec((1, D_MODEL), const),
            weight(D_MODEL, N_BRANCHES * D_MODEL),
            tile(CONV_WIDTH),
            tile(CONV_WIDTH),
            pl.BlockSpec((BF16_SUBLANES, CONV_WIDTH), lambda i: (jnp.maximum(i * hb - 1, 0), 0)),
            pl.BlockSpec((BF16_SUBLANES, CONV_WIDTH), lambda i: (jnp.minimum((i + 1) * hb, n_hb - 1), 0)),
            pl.BlockSpec((CONV_K, CONV_WIDTH), const),
            tile(FOURIER_WIDTH),
            tile(ATTN_WIDTH),
            weight(CONV_WIDTH, D_MODEL),
            weight(FOURIER_WIDTH, D_MODEL),
            weight(ATTN_WIDTH, D_MODEL),
            weight(D_MODEL, D_MODEL),
            weight(D_MODEL, D_FF),
            weight(D_FF, D_MODEL),
        ],
        out_specs=tile(D_MODEL),
        compiler_params=_params(("arbitrary",), 56),
        name="merge_mlp",
    )(x2, mod, g1, g2, wg, bz, u, u, u, wconv, yf, ya, wc, wf, wa, wo, w1, w2)


def _rope_tables(rows):
    n_freq = ROPE_HALF // 2
    inv = ROPE_THETA ** (-jnp.arange(n_freq, dtype=F32) / n_freq)
    row_ang = jnp.repeat(jnp.arange(rows, dtype=F32)[:, None] * inv, GRID_W, axis=0)
    col_ang = jnp.tile(jnp.arange(GRID_W, dtype=F32)[:, None] * inv, (rows, 1))
    ang = jnp.concatenate([row_ang, col_ang], axis=-1)
    cos, sin = jnp.cos(ang), jnp.sin(ang)
    cos_h = jnp.concatenate([cos, cos], axis=-1)
    sin_h = jnp.concatenate([-sin, sin], axis=-1)
    reps = LANES // HEAD_DIM
    return jnp.tile(cos_h, (1, reps)), jnp.tile(sin_h, (1, reps))


def _channel_dft_matrix():
    c, s = _dft_tables(FOURIER_GROUP_DIM)
    eye = jnp.eye(FOURIER_GROUPS, dtype=F32)
    return jnp.concatenate([jnp.kron(eye, c), jnp.kron(eye, -s)], axis=1).astype(BF16)


def kernel(x, c, ctx, c_ctx, w_mod, b_mod, g_norm1, g_norm2, w_in, w_conv, g_q, g_k,
           w_conv_out, w_four_out, w_attn_out, w_o, w_ff1, w_ff2):
    batch, seq, d = x.shape
    ctx_len = ctx.shape[1]
    depth = w_mod.shape[0]
    rows = seq // GRID_W

    cc = jnp.zeros((MOD_ROWS, d), F32).at[:batch].set(c).at[batch].set(c_ctx)
    mods = _mod_call(cc, w_mod, b_mod)

    cos_t, sin_t = _rope_tables(rows)
    cos_c = jnp.ones((ctx_len, LANES), F32)
    sin_c = jnp.zeros((ctx_len, LANES), F32)
    w64 = _channel_dft_matrix()
    w1, wg, wc, wf, wa, wo, wf1, wf2 = _cast_weights(w_in, w_conv_out, w_four_out, w_attn_out, w_o, w_ff1, w_ff2)

    xs = x.reshape(batch * seq, d)
    cs = ctx.reshape(batch * ctx_len, d)
    for l in range(depth):
        mod_x = mods[l, :batch].reshape(batch, N_MOD, d)
        mod_c = mods[l, batch:batch + 1].reshape(1, N_MOD, d)
        g1 = g_norm1[l].reshape(1, d)
        g2 = g_norm2[l].reshape(1, d)
        gqk = jnp.concatenate([jnp.tile(g_q[l], N_Q_HEADS), jnp.tile(g_k[l], N_KV_HEADS)]).reshape(1, QK_WIDTH)
        last = l == depth - 1

        bz_c, u_c, fab_c, qt_c, k_c, vt_c = _in_proj_call(
            cs, mod_c, g1, w1, gqk, cos_c, sin_c, w64,
            layer=l, batch=batch, seq=ctx_len, tm=ctx_len, shared_mod=True)
        k_c3 = k_c.reshape(batch, ctx_len, KV_WIDTH)

        bz, u, fab, qt, k, vt = _in_proj_call(
            xs, mod_x, g1, w1, gqk, cos_t, sin_t, w64,
            layer=l, batch=batch, seq=seq, tm=IN_PROJ_TM, shared_mod=False)
        ya = _attn_call(qt, [(k.reshape(batch, seq, KV_WIDTH), vt), (k_c3, vt_c)],
                        batch=batch, seq=seq, tq=ATTN_TQ, tk=ATTN_TK, q_tiles=ATTN_Q_TILES)
        yf = _fourier_mix(fab, batch=batch, seq=seq)
        xs = _merge_mlp_call(xs, mod_x, g1, g2, wg, bz, u, w_conv[l], yf, ya, wc, wf, wa, wo, wf1, wf2,
                             layer=l, batch=batch, seq=seq, tm=MERGE_TM, shared_mod=False)

        if not last:
            ya_c = _attn_call(qt_c, [(k_c3, vt_c)], batch=batch, seq=ctx_len, tq=ATTN_TQ, tk=ATTN_TK,
                              q_tiles=ctx_len // ATTN_TQ)
            yf_c = _fourier_mix(fab_c, batch=batch, seq=ctx_len)
            cs = _merge_mlp_call(cs, mod_c, g1, g2, wg, bz_c, u_c, w_conv[l], yf_c, ya_c, wc, wf, wa, wo, wf1, wf2,
                                 layer=l, batch=batch, seq=ctx_len, tm=ctx_len, shared_mod=True)
    return xs.reshape(batch, seq, d)
```

```python
import functools

import jax
import jax.numpy as jnp
from jax import lax
from jax.experimental import pallas as pl
from jax.experimental.pallas import tpu as pltpu

D_MODEL = 1024
GRID_W = 64
HEAD_DIM = 64
N_Q_HEADS = 8
N_KV_HEADS = 2
GQA_GROUP = N_Q_HEADS // N_KV_HEADS
ATTN_WIDTH = N_Q_HEADS * HEAD_DIM
KV_WIDTH = N_KV_HEADS * HEAD_DIM
CONV_WIDTH = 256
CONV_K = 3
FOURIER_GROUPS = 4
FOURIER_GROUP_DIM = 64
FOURIER_WIDTH = FOURIER_GROUPS * FOURIER_GROUP_DIM
N_BRANCHES = 3
D_FF = 4 * D_MODEL
ROPE_THETA = 10000.0
ROPE_HALF = HEAD_DIM // 2
EPS = 1e-6
N_MOD = 6

OFF_B = 0
OFF_C = OFF_B + CONV_WIDTH
OFF_X = OFF_C + CONV_WIDTH
OFF_F = OFF_X + CONV_WIDTH
OFF_Q = OFF_F + FOURIER_WIDTH
OFF_K = OFF_Q + ATTN_WIDTH
OFF_V = OFF_K + KV_WIDTH
OFF_G = OFF_V + KV_WIDTH

LANES = 128
SUBLANES = 8
BF16_SUBLANES = 16
QK_WIDTH = ATTN_WIDTH + KV_WIDTH
V_ROWS = HEAD_DIM + BF16_SUBLANES
Q_SCALE = HEAD_DIM ** -0.5 * 1.4426950408889634
ATTN_TQ = 256
ATTN_TK = 512
ATTN_Q_TILES = 2
CAST_STEPS = 8
IN_PROJ_TM = 1024
MERGE_TM = 512
IN_PROJ_SUB = 256
MERGE_SUB = 256
MOD_ROWS = 8

BF16 = jnp.bfloat16
F32 = jnp.float32


def _dot(a, b):
    return jnp.dot(a, b, preferred_element_type=F32)


def _params(semantics, vmem_mb):
    return pltpu.CompilerParams(dimension_semantics=semantics, vmem_limit_bytes=vmem_mb << 20)


def _modulate(x, g, shift, scale):
    ms = jnp.mean(x * x, axis=-1, keepdims=True)
    y = x * lax.rsqrt(ms + EPS)
    return (y * g) * (1.0 + scale) + shift


def _sigmoid(x):
    return 1.0 / (1.0 + jnp.exp(-x))


def _strided_pitch(n):
    tiles = -(-n // SUBLANES)
    return SUBLANES * (tiles if tiles % 2 else tiles + 1)


def _tiled_store(scr, rows, value):
    for j in range(scr.shape[0]):
        scr[j, rows, :] = value[:, j * LANES:(j + 1) * LANES]


def _tiled_load(scr, rows):
    return jnp.concatenate([scr[j, rows, :] for j in range(scr.shape[0])], axis=-1)


def _mod_kernel(c_ref, w_ref, b_ref, o_ref):
    c = c_ref[...]
    s = (c * _sigmoid(c)).astype(BF16)
    o_ref[...] = _dot(s, w_ref[...].astype(BF16)) + b_ref[...]


def _mod_call(cc, w_mod, b_mod):
    depth = w_mod.shape[0]
    tn = 1024
    return pl.pallas_call(
        _mod_kernel,
        out_shape=jax.ShapeDtypeStruct((depth, MOD_ROWS, N_MOD * D_MODEL), F32),
        grid=(depth, N_MOD * D_MODEL // tn),
        in_specs=[
            pl.BlockSpec((MOD_ROWS, D_MODEL), lambda l, j: (0, 0)),
            pl.BlockSpec((None, D_MODEL, tn), lambda l, j: (l, 0, j)),
            pl.BlockSpec((None, 1, tn), lambda l, j: (l, 0, j)),
        ],
        out_specs=pl.BlockSpec((None, MOD_ROWS, tn), lambda l, j: (l, 0, j)),
        compiler_params=_params(("arbitrary", "arbitrary"), 32),
        name="mod",
    )(cc, w_mod, b_mod.reshape(depth, 1, N_MOD * D_MODEL))


def _cast_kernel(*refs):
    n = len(refs) // 2
    for src, dst in zip(refs[:n], refs[n:]):
        dst[...] = src[...].astype(BF16)


def _cast_weights(w_in, w_conv_out, w_four_out, w_attn_out, w_o):
    depth = w_in.shape[0]
    ins = (w_in, w_conv_out, w_four_out, w_attn_out, w_o)
    cols = (OFF_G,) + tuple(w.shape[2] for w in ins[1:])
    spec = lambda w, c: pl.BlockSpec((None, w.shape[1] // CAST_STEPS, c), lambda l, i: (l, i, 0))
    return pl.pallas_call(
        _cast_kernel,
        out_shape=tuple(jax.ShapeDtypeStruct((depth, w.shape[1], c), BF16) for w, c in zip(ins, cols)),
        grid=(depth, CAST_STEPS),
        in_specs=[spec(w, c) for w, c in zip(ins, cols)],
        out_specs=tuple(spec(w, c) for w, c in zip(ins, cols)),
        compiler_params=_params(("arbitrary", "arbitrary"), 40),
        name="cast_weights",
    )(*ins)


def _in_proj_kernel(x_ref, mod_ref, g1_ref, w_ref, gqk_ref, cos_ref, sin_ref, w64_ref,
                    bz_ref, u_ref, fab_ref, qt_ref, k_ref, vt_ref, *fab_scr, sub):
    tm = x_ref.shape[0]
    pitch = _strided_pitch(GRID_W)
    for r0 in range(0, tm, sub):
        rows = slice(r0, r0 + sub)
        h = _modulate(x_ref[rows, :], g1_ref[...], mod_ref[0:1, :], mod_ref[1:2, :]).astype(BF16)
        z = _dot(h, w_ref[...])
        bz_ref[rows, :] = z[:, OFF_B:OFF_C].astype(BF16)
        u_ref[rows, :] = (z[:, OFF_C:OFF_X] * z[:, OFF_X:OFF_F]).astype(BF16)
        fab = _dot(z[:, OFF_F:OFF_Q].astype(BF16), w64_ref[...])
        if fab_scr:
            for a in range(sub // GRID_W):
                start = (r0 // GRID_W + a) * pitch
                _tiled_store(fab_scr[0], slice(start, start + GRID_W), fab[a * GRID_W:(a + 1) * GRID_W, :])
        else:
            fab_ref[rows, :] = fab.astype(BF16)

        lane = lax.broadcasted_iota(jnp.int32, (sub, LANES), 1)
        head_lo = lane < HEAD_DIM
        half_lo = (lane % HEAD_DIM) < ROPE_HALF
        cos = cos_ref[rows, :]
        sin = sin_ref[rows, :]
        for j in range(QK_WIDTH // LANES):
            zj = z[:, OFF_Q + j * LANES:OFF_Q + (j + 1) * LANES]
            sq = zj * zj
            s_lo = jnp.sum(jnp.where(head_lo, sq, 0.0), axis=-1, keepdims=True)
            s_hi = jnp.sum(jnp.where(head_lo, 0.0, sq), axis=-1, keepdims=True)
            r = jnp.where(head_lo, lax.rsqrt(s_lo / HEAD_DIM + EPS), lax.rsqrt(s_hi / HEAD_DIM + EPS))
            n = (zj * r) * gqk_ref[:, j * LANES:(j + 1) * LANES]
            partner = jnp.where(half_lo, pltpu.roll(n, LANES - ROPE_HALF, 1), pltpu.roll(n, ROPE_HALF, 1))
            rot = n * cos + partner * sin
            if j < ATTN_WIDTH // LANES:
                qt_ref[j * LANES:(j + 1) * LANES, rows] = (rot * Q_SCALE).T.astype(BF16)
            else:
                k_ref[rows, :] = rot.astype(BF16)
        vt = z[:, OFF_V:OFF_G].T.astype(BF16)
        for g in range(N_KV_HEADS):
            vt_ref[g * V_ROWS:g * V_ROWS + HEAD_DIM, rows] = vt[g * HEAD_DIM:(g + 1) * HEAD_DIM, :]
            vt_ref[g * V_ROWS + HEAD_DIM:(g + 1) * V_ROWS, rows] = jnp.ones((V_ROWS - HEAD_DIM, sub), BF16)
    if fab_scr:
        for n2 in range(GRID_W):
            fab_ref[n2] = _tiled_load(fab_scr[0], pl.ds(n2, tm // GRID_W, stride=pitch)).astype(BF16)


def _in_proj_call(x2, mod, g1, w1, gqk, cos_t, sin_t, w64, *, layer, batch, seq, tm, shared_mod):
    n = batch * seq
    tps = seq // tm
    mod_idx = (lambda i: (0, 0, 0)) if shared_mod else (lambda i: (i // tps, 0, 0))
    const = lambda i: (0, 0)
    tile_rows = tm // GRID_W
    column_major = tile_rows >= BF16_SUBLANES
    if column_major:
        fab_shape = jax.ShapeDtypeStruct((batch, GRID_W, seq // GRID_W, 2 * FOURIER_WIDTH), BF16)
        fab_spec = pl.BlockSpec((None, GRID_W, tile_rows, 2 * FOURIER_WIDTH), lambda i: (i // tps, 0, i % tps, 0))
        scratch = [pltpu.VMEM((2 * FOURIER_WIDTH // LANES, tile_rows * _strided_pitch(GRID_W), LANES), F32)]
    else:
        fab_shape = jax.ShapeDtypeStruct((n, 2 * FOURIER_WIDTH), BF16)
        fab_spec = pl.BlockSpec((tm, 2 * FOURIER_WIDTH), lambda i: (i, 0))
        scratch = []
    return pl.pallas_call(
        functools.partial(_in_proj_kernel, sub=min(tm, IN_PROJ_SUB)),
        out_shape=(
            jax.ShapeDtypeStruct((n, CONV_WIDTH), BF16),
            jax.ShapeDtypeStruct((n, CONV_WIDTH), BF16),
            fab_shape,
            jax.ShapeDtypeStruct((batch, ATTN_WIDTH, seq), BF16),
            jax.ShapeDtypeStruct((n, KV_WIDTH), BF16),
            jax.ShapeDtypeStruct((batch, N_KV_HEADS * V_ROWS, seq), BF16),
        ),
        grid=(n // tm,),
        in_specs=[
            pl.BlockSpec((tm, D_MODEL), lambda i: (i, 0)),
            pl.BlockSpec((None, N_MOD, D_MODEL), mod_idx),
            pl.BlockSpec((1, D_MODEL), const),
            pl.BlockSpec((None, D_MODEL, OFF_G), lambda i: (layer, 0, 0)),
            pl.BlockSpec((1, QK_WIDTH), const),
            pl.BlockSpec((tm, LANES), lambda i: (i % tps, 0)),
            pl.BlockSpec((tm, LANES), lambda i: (i % tps, 0)),
            pl.BlockSpec((FOURIER_WIDTH, 2 * FOURIER_WIDTH), const),
        ],
        out_specs=(
            pl.BlockSpec((tm, CONV_WIDTH), lambda i: (i, 0)),
            pl.BlockSpec((tm, CONV_WIDTH), lambda i: (i, 0)),
            fab_spec,
            pl.BlockSpec((None, ATTN_WIDTH, tm), lambda i: (i // tps, 0, i % tps)),
            pl.BlockSpec((tm, KV_WIDTH), lambda i: (i, 0)),
            pl.BlockSpec((None, N_KV_HEADS * V_ROWS, tm), lambda i: (i // tps, 0, i % tps)),
        ),
        scratch_shapes=scratch,
        compiler_params=_params(("arbitrary",), 48),
        name="in_proj",
    )(x2, mod, g1, w1, gqk, cos_t, sin_t, w64)


def _fft_kernel(z_ref, cr_ref, sr_ref, twc_ref, tws_ref, cc_ref, sc_ref, y_ref, g_scr, y_scr):
    w = FOURIER_WIDTH
    rows = z_ref.shape[1]
    pitch = _strided_pitch(rows)
    for n2 in range(GRID_W):
        zb = z_ref[n2]
        p = _dot(cr_ref[...], zb)
        q = _dot(sr_ref[...], zb)
        gr = p[:, :w] + q[:, w:]
        gi = p[:, w:] - q[:, :w]
        tc = twc_ref[n2]
        ts = tws_ref[n2]
        g = jnp.concatenate([gr * tc + gi * ts, gi * tc - gr * ts], axis=-1)
        _tiled_store(g_scr, slice(n2 * pitch, n2 * pitch + rows), g)
    for k1 in range(rows):
        slab = _tiled_load(g_scr, pl.ds(k1, GRID_W, stride=pitch)).astype(BF16)
        y = _dot(cc_ref[...], slab[:, :w]) + _dot(sc_ref[...], slab[:, w:])
        _tiled_store(y_scr, pl.ds(k1, GRID_W, stride=pitch), y)
    for k2 in range(GRID_W):
        y_ref[k2 * rows:(k2 + 1) * rows, :] = _tiled_load(y_scr, slice(k2 * pitch, k2 * pitch + rows)).astype(BF16)


def _fft_call(zt, c_r, s_r, twc, tws, c_c, s_c, *, batch, rows):
    seq = rows * GRID_W
    const2 = lambda b: (0, 0)
    const3 = lambda b: (0, 0, 0)
    return pl.pallas_call(
        _fft_kernel,
        out_shape=jax.ShapeDtypeStruct((batch * seq, FOURIER_WIDTH), BF16),
        grid=(batch,),
        in_specs=[
            pl.BlockSpec((None, GRID_W, rows, 2 * FOURIER_WIDTH), lambda b: (b, 0, 0, 0)),
            pl.BlockSpec((rows, rows), const2),
            pl.BlockSpec((rows, rows), const2),
            pl.BlockSpec((GRID_W, rows, 1), const3),
            pl.BlockSpec((GRID_W, rows, 1), const3),
            pl.BlockSpec((GRID_W, GRID_W), const2),
            pl.BlockSpec((GRID_W, GRID_W), const2),
        ],
        out_specs=pl.BlockSpec((seq, FOURIER_WIDTH), lambda b: (b, 0)),
        scratch_shapes=[pltpu.VMEM((2 * FOURIER_WIDTH // LANES, GRID_W * _strided_pitch(rows), LANES), F32),
                        pltpu.VMEM((FOURIER_WIDTH // LANES, GRID_W * _strided_pitch(rows), LANES), F32)],
        compiler_params=_params(("arbitrary",), 40),
        name="fft",
    )(zt, c_r, s_r, twc, tws, c_c, s_c)


def _dft_tables(n):
    idx = jnp.arange(n, dtype=jnp.int32)
    ang = (2.0 * jnp.pi / n) * ((idx[:, None] * idx[None, :]) % n).astype(F32)
    return jnp.cos(ang), jnp.sin(ang)


def _dft_dense_kernel(z_ref, c_ref, s_ref, y_ref):
    w = FOURIER_WIDTH
    y_ref[...] = (_dot(c_ref[...], z_ref[:, :w]) + _dot(s_ref[...], z_ref[:, w:])).astype(BF16)


def _dft_dense_call(fab, c_l, s_l, *, batch, seq):
    return pl.pallas_call(
        _dft_dense_kernel,
        out_shape=jax.ShapeDtypeStruct((batch * seq, FOURIER_WIDTH), BF16),
        grid=(batch,),
        in_specs=[
            pl.BlockSpec((seq, 2 * FOURIER_WIDTH), lambda b: (b, 0)),
            pl.BlockSpec((seq, seq), lambda b: (0, 0)),
            pl.BlockSpec((seq, seq), lambda b: (0, 0)),
        ],
        out_specs=pl.BlockSpec((seq, FOURIER_WIDTH), lambda b: (b, 0)),
        compiler_params=_params(("arbitrary",), 32),
        name="dft_dense",
    )(fab, c_l, s_l)


def _fourier_mix(fab, *, batch, seq):
    rows = seq // GRID_W
    scale = (seq * FOURIER_GROUP_DIM) ** -0.5
    if rows < BF16_SUBLANES:
        c_l, s_l = _dft_tables(seq)
        return _dft_dense_call(fab, (c_l * scale).astype(BF16), (s_l * scale).astype(BF16), batch=batch, seq=seq)
    c_r, s_r = _dft_tables(rows)
    c_c, s_c = _dft_tables(GRID_W)
    n2 = jnp.arange(GRID_W, dtype=jnp.int32)
    k1 = jnp.arange(rows, dtype=jnp.int32)
    tw_ang = (2.0 * jnp.pi / seq) * (n2[:, None] * k1[None, :]).astype(F32)[:, :, None]
    return _fft_call(fab, c_r.astype(BF16), s_r.astype(BF16), jnp.cos(tw_ang), jnp.sin(tw_ang),
                     (c_c * scale).astype(BF16), (s_c * scale).astype(BF16), batch=batch, rows=rows)


def _attn_kernel(*refs, chunks, tq, cast_cols):
    n_src = len(chunks)
    n_cast = len(cast_cols)
    qt_ref = refs[0]
    kv_refs = refs[1:1 + 2 * n_src]
    cast_in = refs[1 + 2 * n_src:1 + 2 * n_src + n_cast]
    o_ref = refs[1 + 2 * n_src + n_cast]
    cast_out = refs[2 + 2 * n_src + n_cast:2 + 2 * n_src + 2 * n_cast]
    s_scr = refs[-1]
    for src, dst, (c0, c1) in zip(cast_in, cast_out, cast_cols):
        dst[...] = src[:, c0:c1].astype(BF16)
    g = pl.program_id(1)
    row = lax.broadcasted_iota(jnp.int32, (KV_WIDTH, tq), 0)
    own_group = (row >= g * HEAD_DIM) & (row < (g + 1) * HEAD_DIM)
    steps = [(src, c) for src, (n_chunks, _) in enumerate(chunks) for c in range(n_chunks)]

    def k_chunk(src, c):
        tk = chunks[src][1]
        return kv_refs[2 * src][c * tk:(c + 1) * tk, :]

    def v_chunk(src, c):
        tk = chunks[src][1]
        return kv_refs[2 * src + 1][:, c * tk:(c + 1) * tk]

    def scores(qz_h, h, kc, slot):
        n = kc.shape[0]
        s = _dot(kc, qz_h)
        s_scr[slot, h, 0:n, :] = s
        part = jnp.max(s.reshape(n // HEAD_DIM, HEAD_DIM, tq), axis=0)
        return jnp.max(part, axis=0, keepdims=True)

    def accumulate(h, carry_h, chunk_max, vc, slot):
        n = vc.shape[1]
        m, acc = carry_h
        m_new = jnp.maximum(m, chunk_max)
        p = jnp.exp2(s_scr[slot, h, 0:n, :] - m_new).astype(BF16)
        return m_new, jnp.exp2(m - m_new) * acc + _dot(vc, p)

    slot = 0
    for q0 in range(0, qt_ref.shape[1], tq):
        qt = qt_ref[:, q0:q0 + tq]
        qz = [jnp.where(own_group, jnp.concatenate([qt[h * HEAD_DIM:(h + 1) * HEAD_DIM, :]] * N_KV_HEADS, axis=0),
                        jnp.zeros((), BF16)) for h in range(GQA_GROUP)]
        carry = [(jnp.full((1, tq), -jnp.inf, F32), jnp.zeros((V_ROWS, tq), F32)) for _ in range(GQA_GROUP)]
        first_chunk = k_chunk(*steps[0])
        maxima = [scores(qz[h], h, first_chunk, slot) for h in range(GQA_GROUP)]
        for t, cur in enumerate(steps):
            vc = v_chunk(*cur)
            kc = k_chunk(*steps[t + 1]) if t + 1 < len(steps) else None
            next_maxima = []
            for h in range(GQA_GROUP):
                if kc is not None:
                    next_maxima.append(scores(qz[h], h, kc, 1 - slot))
                carry[h] = accumulate(h, carry[h], maxima[h], vc, slot)
            maxima = next_maxima
            slot = 1 - slot
        outs = [acc[:HEAD_DIM, :] / acc[HEAD_DIM:HEAD_DIM + 1, :] for _, acc in carry]
        o_ref[q0:q0 + tq, :] = jnp.concatenate(outs, axis=0).T.astype(BF16)


def _attn_call(qt, sources, *, batch, seq, tq, tk, q_tiles, cast=(), layer=0):
    bq = tq * q_tiles
    nq = seq // bq
    gw = GQA_GROUP * HEAD_DIM
    n_steps = batch * N_KV_HEADS * nq
    step = lambda b, g, i: (b * N_KV_HEADS + g) * nq + i
    in_specs = [pl.BlockSpec((None, gw, bq), lambda b, g, i: (b, g, i))]
    args = [qt]
    chunks = []
    for k, vt in sources:
        ln = k.shape[1]
        ck = min(tk, ln)
        chunks.append((ln // ck, ck))
        in_specs.append(pl.BlockSpec((None, ln, KV_WIDTH), lambda b, g, i: (b, 0, 0)))
        in_specs.append(pl.BlockSpec((None, V_ROWS, ln), lambda b, g, i: (b, g, 0)))
        args += [k, vt]
    out_shape = [jax.ShapeDtypeStruct((batch * seq, ATTN_WIDTH), BF16)]
    out_specs = [pl.BlockSpec((bq, gw), lambda b, g, i: (b * nq + i, g))]
    for w, c0, c1 in cast:
        rows = w.shape[1] // n_steps
        assert rows * n_steps == w.shape[1] and rows % BF16_SUBLANES == 0, w.shape
        in_specs.append(pl.BlockSpec((None, rows, w.shape[2]), lambda b, g, i: (layer, step(b, g, i), 0)))
        args.append(w)
        out_shape.append(jax.ShapeDtypeStruct((w.shape[1], c1 - c0), BF16))
        out_specs.append(pl.BlockSpec((rows, c1 - c0), lambda b, g, i: (step(b, g, i), 0)))
    return pl.pallas_call(
        functools.partial(_attn_kernel, chunks=tuple(chunks), tq=tq, cast_cols=tuple((c0, c1) for _, c0, c1 in cast)),
        out_shape=tuple(out_shape),
        grid=(batch, N_KV_HEADS, nq),
        in_specs=in_specs,
        out_specs=tuple(out_specs),
        scratch_shapes=[pltpu.VMEM((2, GQA_GROUP, max(ck for _, ck in chunks), tq), F32)],
        compiler_params=_params(("arbitrary", "arbitrary", "arbitrary"), 32),
        name="attention",
    )(*args)


def _merge_mlp_kernel(x_ref, mod_ref, g1_ref, g2_ref, wg_ref, bz_ref, u_ref, up_ref, un_ref, wconv_ref,
                      yf_ref, ya_ref, wc_ref, wf_ref, wa_ref, wo_ref, w1_ref, w2_ref, o_ref, *, tps, sub, ff_chunk):
    tm = x_ref.shape[0]

    pos = pl.program_id(0) % tps
    u = u_ref[...].astype(F32)
    prev_row = jnp.where(pos == 0, 0.0, up_ref[...].astype(F32)[BF16_SUBLANES - 1:BF16_SUBLANES, :])
    next_row = jnp.where(pos == tps - 1, 0.0, un_ref[...].astype(F32)[0:1, :])
    row = lax.broadcasted_iota(jnp.int32, u.shape, 0)
    u_m1 = jnp.where(row == 0, prev_row, pltpu.roll(u, 1, 0))
    u_p1 = jnp.where(row == tm - 1, next_row, pltpu.roll(u, tm - 1, 0))
    conv = wconv_ref[0:1, :] * u_m1 + wconv_ref[1:2, :] * u + wconv_ref[2:3, :] * u_p1
    y_conv = (bz_ref[...].astype(F32) * conv).astype(BF16)

    d = D_MODEL
    for r0 in range(0, tm, sub):
        rows = slice(r0, r0 + sub)
        x = x_ref[rows, :]
        h = _modulate(x, g1_ref[...], mod_ref[0:1, :], mod_ref[1:2, :]).astype(BF16)
        m = _sigmoid(_dot(h, wg_ref[:, 0:d])) * _dot(y_conv[rows, :], wc_ref[...])
        m = m + _sigmoid(_dot(h, wg_ref[:, d:2 * d])) * _dot(yf_ref[rows, :], wf_ref[...])
        m = m + _sigmoid(_dot(h, wg_ref[:, 2 * d:3 * d])) * _dot(ya_ref[rows, :], wa_ref[...])
        x1 = x + mod_ref[2:3, :] * _dot(m.astype(BF16), wo_ref[...])

        h2 = _modulate(x1, g2_ref[...], mod_ref[3:4, :], mod_ref[4:5, :]).astype(BF16)
        acc = jnp.zeros(x1.shape, F32)
        for c in range(D_FF // ff_chunk):
            a = jnp.maximum(_dot(h2, w1_ref[:, c * ff_chunk:(c + 1) * ff_chunk]), 0.0)
            acc = acc + _dot((a * a).astype(BF16), w2_ref[c * ff_chunk:(c + 1) * ff_chunk, :])
        o_ref[rows, :] = x1 + mod_ref[5:6, :] * acc


def _merge_mlp_call(x2, mod, g1, g2, wg, bz, u, wconv, yf, ya, wc, wf, wa, wo, w1, w2, *,
                    layer, batch, seq, tm, shared_mod):
    n = batch * seq
    tps = seq // tm
    hb = tm // BF16_SUBLANES
    n_hb = n // BF16_SUBLANES
    mod_idx = (lambda i: (0, 0, 0)) if shared_mod else (lambda i: (i // tps, 0, 0))
    const = lambda i: (0, 0)
    tile = lambda w: pl.BlockSpec((tm, w), lambda i: (i, 0))
    def weight(w):
        if w.ndim == 2:
            return pl.BlockSpec(w.shape, const, pipeline_mode=pl.Buffered(1))
        return pl.BlockSpec((None,) + w.shape[1:], lambda i: (layer, 0, 0), pipeline_mode=pl.Buffered(1))
    return pl.pallas_call(
        functools.partial(_merge_mlp_kernel, tps=tps, sub=min(tm, MERGE_SUB), ff_chunk=1024),
        out_shape=jax.ShapeDtypeStruct((n, D_MODEL), F32),
        grid=(n // tm,),
        in_specs=[
            tile(D_MODEL),
            pl.BlockSpec((None, N_MOD, D_MODEL), mod_idx),
            pl.BlockSpec((1, D_MODEL), const),
            pl.BlockSpec((1, D_MODEL), const),
            weight(wg),
            tile(CONV_WIDTH),
            tile(CONV_WIDTH),
            pl.BlockSpec((BF16_SUBLANES, CONV_WIDTH), lambda i: (jnp.maximum(i * hb - 1, 0), 0)),
            pl.BlockSpec((BF16_SUBLANES, CONV_WIDTH), lambda i: (jnp.minimum((i + 1) * hb, n_hb - 1), 0)),
            pl.BlockSpec((CONV_K, CONV_WIDTH), const),
            tile(FOURIER_WIDTH),
            tile(ATTN_WIDTH),
            weight(wc),
            weight(wf),
            weight(wa),
            weight(wo),
            weight(w1),
            weight(w2),
        ],
        out_specs=tile(D_MODEL),
        compiler_params=_params(("arbitrary",), 56),
        name="merge_mlp",
    )(x2, mod, g1, g2, wg, bz, u, u, u, wconv, yf, ya, wc, wf, wa, wo, w1, w2)


def _rope_tables(rows):
    n_freq = ROPE_HALF // 2
    inv = ROPE_THETA ** (-jnp.arange(n_freq, dtype=F32) / n_freq)
    row_ang = jnp.repeat(jnp.arange(rows, dtype=F32)[:, None] * inv, GRID_W, axis=0)
    col_ang = jnp.tile(jnp.arange(GRID_W, dtype=F32)[:, None] * inv, (rows, 1))
    ang = jnp.concatenate([row_ang, col_ang], axis=-1)
    cos, sin = jnp.cos(ang), jnp.sin(ang)
    cos_h = jnp.concatenate([cos, cos], axis=-1)
    sin_h = jnp.concatenate([-sin, sin], axis=-1)
    reps = LANES // HEAD_DIM
    return jnp.tile(cos_h, (1, reps)), jnp.tile(sin_h, (1, reps))


def _channel_dft_matrix():
    c, s = _dft_tables(FOURIER_GROUP_DIM)
    eye = jnp.eye(FOURIER_GROUPS, dtype=F32)
    return jnp.concatenate([jnp.kron(eye, c), jnp.kron(eye, -s)], axis=1).astype(BF16)


def kernel(x, c, ctx, c_ctx, w_mod, b_mod, g_norm1, g_norm2, w_in, w_conv, g_q, g_k,
           w_conv_out, w_four_out, w_attn_out, w_o, w_ff1, w_ff2):
    batch, seq, d = x.shape
    ctx_len = ctx.shape[1]
    depth = w_mod.shape[0]
    rows = seq // GRID_W

    cc = jnp.zeros((MOD_ROWS, d), F32).at[:batch].set(c).at[batch].set(c_ctx)
    mods = _mod_call(cc, w_mod, b_mod)

    cos_t, sin_t = _rope_tables(rows)
    cos_c = jnp.ones((ctx_len, LANES), F32)
    sin_c = jnp.zeros((ctx_len, LANES), F32)
    w64 = _channel_dft_matrix()
    w1, wc, wf, wa, wo = _cast_weights(w_in, w_conv_out, w_four_out, w_attn_out, w_o)
    big_weights = ((w_in, OFF_G, w_in.shape[2]), (w_ff1, 0, w_ff1.shape[2]), (w_ff2, 0, w_ff2.shape[2]))

    xs = x.reshape(batch * seq, d)
    cs = ctx.reshape(batch * ctx_len, d)
    for l in range(depth):
        mod_x = mods[l, :batch].reshape(batch, N_MOD, d)
        mod_c = mods[l, batch:batch + 1].reshape(1, N_MOD, d)
        g1 = g_norm1[l].reshape(1, d)
        g2 = g_norm2[l].reshape(1, d)
        gqk = jnp.concatenate([jnp.tile(g_q[l], N_Q_HEADS), jnp.tile(g_k[l], N_KV_HEADS)]).reshape(1, QK_WIDTH)
        last = l == depth - 1

        bz_c, u_c, fab_c, qt_c, k_c, vt_c = _in_proj_call(
            cs, mod_c, g1, w1, gqk, cos_c, sin_c, w64,
            layer=l, batch=batch, seq=ctx_len, tm=ctx_len, shared_mod=True)
        k_c3 = k_c.reshape(batch, ctx_len, KV_WIDTH)

        bz, u, fab, qt, k, vt = _in_proj_call(
            xs, mod_x, g1, w1, gqk, cos_t, sin_t, w64,
            layer=l, batch=batch, seq=seq, tm=IN_PROJ_TM, shared_mod=False)
        ya, wg, wf1, wf2 = _attn_call(qt, [(k.reshape(batch, seq, KV_WIDTH), vt), (k_c3, vt_c)],
                                      batch=batch, seq=seq, tq=ATTN_TQ, tk=ATTN_TK, q_tiles=ATTN_Q_TILES,
                                      cast=big_weights, layer=l)
        yf = _fourier_mix(fab, batch=batch, seq=seq)
        xs = _merge_mlp_call(xs, mod_x, g1, g2, wg, bz, u, w_conv[l], yf, ya, wc, wf, wa, wo, wf1, wf2,
                             layer=l, batch=batch, seq=seq, tm=MERGE_TM, shared_mod=False)

        if not last:
            (ya_c,) = _attn_call(qt_c, [(k_c3, vt_c)], batch=batch, seq=ctx_len, tq=ATTN_TQ, tk=ATTN_TK,
                                 q_tiles=ctx_len // ATTN_TQ)
            yf_c = _fourier_mix(fab_c, batch=batch, seq=ctx_len)
            cs = _merge_mlp_call(cs, mod_c, g1, g2, wg, bz_c, u_c, w_conv[l], yf_c, ya_c, wc, wf, wa, wo, wf1, wf2,
                                 layer=l, batch=batch, seq=ctx_len, tm=ctx_len, shared_mod=True)
    return xs.reshape(batch, seq, d)
```

```python
import functools
import math

import jax
import jax.numpy as jnp
from jax import lax
from jax.experimental import pallas as pl
from jax.experimental.pallas import tpu as pltpu

D_MODEL = 1024
GRID_W = 64
HEAD_DIM = 64
N_Q_HEADS = 8
N_KV_HEADS = 2
GQA_GROUP = N_Q_HEADS // N_KV_HEADS
ATTN_WIDTH = N_Q_HEADS * HEAD_DIM
KV_WIDTH = N_KV_HEADS * HEAD_DIM
CONV_WIDTH = 256
CONV_K = 3
FOURIER_GROUPS = 4
FOURIER_GROUP_DIM = 64
FOURIER_WIDTH = FOURIER_GROUPS * FOURIER_GROUP_DIM
N_BRANCHES = 3
D_FF = 4 * D_MODEL
ROPE_THETA = 10000.0
ROPE_HALF = HEAD_DIM // 2
EPS = 1e-6
N_MOD = 6

OFF_B = 0
OFF_C = OFF_B + CONV_WIDTH
OFF_X = OFF_C + CONV_WIDTH
OFF_F = OFF_X + CONV_WIDTH
OFF_Q = OFF_F + FOURIER_WIDTH
OFF_K = OFF_Q + ATTN_WIDTH
OFF_V = OFF_K + KV_WIDTH
OFF_G = OFF_V + KV_WIDTH

LANES = 128
SUBLANES = 8
BF16_SUBLANES = 16
QK_WIDTH = ATTN_WIDTH + KV_WIDTH
V_ROWS = HEAD_DIM + BF16_SUBLANES
Q_SCALE = HEAD_DIM ** -0.5 * 1.4426950408889634
ATTN_TQ = 256
ATTN_TK = 512
ATTN_Q_TILES = 2
MOD_TN = 1024
FF_CHUNK = 1024
CAST_STEPS = 8
IN_PROJ_TM = 2048
MERGE_TM = 512
IN_PROJ_SUB = 256
MERGE_SUB = 256
MOD_ROWS = 8

BF16 = jnp.bfloat16
F32 = jnp.float32


def _dot(a, b):
    return jnp.dot(a, b, preferred_element_type=F32)


def _tile_bytes(shape, dtype):
    itemsize = jnp.dtype(dtype).itemsize
    dims = [1 if d is None else d for d in shape]
    sublanes = SUBLANES * (4 // itemsize)
    dims[-1] = -(-dims[-1] // LANES) * LANES
    if len(dims) > 1:
        dims[-2] = -(-dims[-2] // sublanes) * sublanes
    return math.prod(dims) * itemsize


def _pallas(kernel, *, name, grid, in_specs, out_specs, out_shape, args, scratch=(), temporaries=()):
    outs = out_shape if isinstance(out_shape, tuple) else (out_shape,)
    ospecs = out_specs if isinstance(out_specs, tuple) else (out_specs,)
    total = 0
    for spec, dtype in [(s, a.dtype) for s, a in zip(in_specs, args)] + [(s, o.dtype) for s, o in zip(ospecs, outs)]:
        buffers = spec.pipeline_mode.buffer_count if spec.pipeline_mode is not None else 2
        total += buffers * _tile_bytes(spec.block_shape, dtype)
    total += sum(_tile_bytes(s.shape, s.dtype) for s in scratch)
    total += sum(_tile_bytes(shape, dtype) for shape, dtype in temporaries)
    return pl.pallas_call(
        kernel, out_shape=out_shape, grid=grid, in_specs=in_specs, out_specs=out_specs, scratch_shapes=list(scratch),
        compiler_params=pltpu.CompilerParams(dimension_semantics=("arbitrary",) * len(grid), vmem_limit_bytes=total),
        name=name,
    )(*args)


def _modulate(x, g, shift, scale):
    ms = jnp.mean(x * x, axis=-1, keepdims=True)
    y = x * lax.rsqrt(ms + EPS)
    return (y * g) * (1.0 + scale) + shift


def _sigmoid(x):
    return 1.0 / (1.0 + jnp.exp(-x))


def _strided_pitch(n):
    tiles = -(-n // SUBLANES)
    return SUBLANES * (tiles if tiles % 2 else tiles + 1)


def _tiled_store(scr, rows, value):
    for j in range(scr.shape[0]):
        scr[j, rows, :] = value[:, j * LANES:(j + 1) * LANES]


def _tiled_load(scr, rows):
    return jnp.concatenate([scr[j, rows, :] for j in range(scr.shape[0])], axis=-1)


def _mod_kernel(c_ref, w_ref, b_ref, o_ref):
    c = c_ref[...]
    s = (c * _sigmoid(c)).astype(BF16)
    o_ref[...] = _dot(s, w_ref[...].astype(BF16)) + b_ref[...]


def _mod_call(cc, w_mod, b_mod):
    depth = w_mod.shape[0]
    tn = MOD_TN
    return _pallas(
        _mod_kernel,
        name="mod",
        out_shape=jax.ShapeDtypeStruct((depth, MOD_ROWS, N_MOD * D_MODEL), F32),
        grid=(depth, N_MOD * D_MODEL // tn),
        in_specs=[
            pl.BlockSpec((MOD_ROWS, D_MODEL), lambda l, j: (0, 0)),
            pl.BlockSpec((None, D_MODEL, tn), lambda l, j: (l, 0, j)),
            pl.BlockSpec((None, 1, tn), lambda l, j: (l, 0, j)),
        ],
        out_specs=pl.BlockSpec((None, MOD_ROWS, tn), lambda l, j: (l, 0, j)),
        args=(cc, w_mod, b_mod.reshape(depth, 1, N_MOD * D_MODEL)),
        temporaries=[((D_MODEL, tn), BF16)],
    )


def _cast_kernel(*refs):
    n = len(refs) // 2
    for src, dst in zip(refs[:n], refs[n:]):
        dst[...] = src[...].astype(BF16)


def _cast_weights(w_in, w_conv_out, w_four_out, w_attn_out, w_o):
    depth = w_in.shape[0]
    ins = (w_in, w_conv_out, w_four_out, w_attn_out, w_o)
    cols = (OFF_G,) + tuple(w.shape[2] for w in ins[1:])
    spec = lambda w, c: pl.BlockSpec((None, w.shape[1] // CAST_STEPS, c), lambda l, i: (l, i, 0))
    return _pallas(
        _cast_kernel,
        name="cast_weights",
        out_shape=tuple(jax.ShapeDtypeStruct((depth, w.shape[1], c), BF16) for w, c in zip(ins, cols)),
        grid=(depth, CAST_STEPS),
        in_specs=[spec(w, c) for w, c in zip(ins, cols)],
        out_specs=tuple(spec(w, c) for w, c in zip(ins, cols)),
        args=ins,
    )


def _in_proj_kernel(x_ref, mod_ref, g1_ref, w_ref, gqk_ref, cos_ref, sin_ref, w64_ref,
                    bz_ref, u_ref, fab_ref, qt_ref, k_ref, vt_ref, *fab_scr, sub):
    tm = x_ref.shape[0]
    pitch = _strided_pitch(GRID_W)
    for r0 in range(0, tm, sub):
        rows = slice(r0, r0 + sub)
        h = _modulate(x_ref[rows, :], g1_ref[...], mod_ref[0:1, :], mod_ref[1:2, :]).astype(BF16)
        z = _dot(h, w_ref[...])
        bz_ref[rows, :] = z[:, OFF_B:OFF_C].astype(BF16)
        u_ref[rows, :] = (z[:, OFF_C:OFF_X] * z[:, OFF_X:OFF_F]).astype(BF16)
        fab = _dot(z[:, OFF_F:OFF_Q].astype(BF16), w64_ref[...])
        if fab_scr:
            for a in range(sub // GRID_W):
                start = (r0 // GRID_W + a) * pitch
                _tiled_store(fab_scr[0], slice(start, start + GRID_W), fab[a * GRID_W:(a + 1) * GRID_W, :])
        else:
            fab_ref[rows, :] = fab.astype(BF16)

        lane = lax.broadcasted_iota(jnp.int32, (sub, LANES), 1)
        head_lo = lane < HEAD_DIM
        half_lo = (lane % HEAD_DIM) < ROPE_HALF
        cos = cos_ref[rows, :]
        sin = sin_ref[rows, :]
        for j in range(QK_WIDTH // LANES):
            zj = z[:, OFF_Q + j * LANES:OFF_Q + (j + 1) * LANES]
            sq = zj * zj
            s_lo = jnp.sum(jnp.where(head_lo, sq, 0.0), axis=-1, keepdims=True)
            s_hi = jnp.sum(jnp.where(head_lo, 0.0, sq), axis=-1, keepdims=True)
            r = jnp.where(head_lo, lax.rsqrt(s_lo / HEAD_DIM + EPS), lax.rsqrt(s_hi / HEAD_DIM + EPS))
            n = (zj * r) * gqk_ref[:, j * LANES:(j + 1) * LANES]
            partner = jnp.where(half_lo, pltpu.roll(n, LANES - ROPE_HALF, 1), pltpu.roll(n, ROPE_HALF, 1))
            rot = n * cos + partner * sin
            if j < ATTN_WIDTH // LANES:
                qt_ref[j * LANES:(j + 1) * LANES, rows] = (rot * Q_SCALE).T.astype(BF16)
            else:
                k_ref[rows, :] = rot.astype(BF16)
        vt = z[:, OFF_V:OFF_G].T.astype(BF16)
        for g in range(N_KV_HEADS):
            vt_ref[g * V_ROWS:g * V_ROWS + HEAD_DIM, rows] = vt[g * HEAD_DIM:(g + 1) * HEAD_DIM, :]
            vt_ref[g * V_ROWS + HEAD_DIM:(g + 1) * V_ROWS, rows] = jnp.ones((V_ROWS - HEAD_DIM, sub), BF16)
    if fab_scr:
        for n2 in range(GRID_W):
            fab_ref[n2] = _tiled_load(fab_scr[0], pl.ds(n2, tm // GRID_W, stride=pitch)).astype(BF16)


def _in_proj_call(x2, mod, g1, w1, gqk, cos_t, sin_t, w64, *, layer, batch, seq, tm, shared_mod):
    n = batch * seq
    tps = seq // tm
    mod_idx = (lambda i: (0, 0, 0)) if shared_mod else (lambda i: (i // tps, 0, 0))
    const = lambda i: (0, 0)
    tile_rows = tm // GRID_W
    column_major = tile_rows >= BF16_SUBLANES
    if column_major:
        fab_shape = jax.ShapeDtypeStruct((batch, GRID_W, seq // GRID_W, 2 * FOURIER_WIDTH), BF16)
        fab_spec = pl.BlockSpec((None, GRID_W, tile_rows, 2 * FOURIER_WIDTH), lambda i: (i // tps, 0, i % tps, 0))
        scratch = [pltpu.VMEM((2 * FOURIER_WIDTH // LANES, tile_rows * _strided_pitch(GRID_W), LANES), F32)]
    else:
        fab_shape = jax.ShapeDtypeStruct((n, 2 * FOURIER_WIDTH), BF16)
        fab_spec = pl.BlockSpec((tm, 2 * FOURIER_WIDTH), lambda i: (i, 0))
        scratch = []
    sub = min(tm, IN_PROJ_SUB)
    return _pallas(
        functools.partial(_in_proj_kernel, sub=sub),
        name="in_proj",
        out_shape=(
            jax.ShapeDtypeStruct((n, CONV_WIDTH), BF16),
            jax.ShapeDtypeStruct((n, CONV_WIDTH), BF16),
            fab_shape,
            jax.ShapeDtypeStruct((batch, ATTN_WIDTH, seq), BF16),
            jax.ShapeDtypeStruct((n, KV_WIDTH), BF16),
            jax.ShapeDtypeStruct((batch, N_KV_HEADS * V_ROWS, seq), BF16),
        ),
        grid=(n // tm,),
        in_specs=[
            pl.BlockSpec((tm, D_MODEL), lambda i: (i, 0)),
            pl.BlockSpec((None, N_MOD, D_MODEL), mod_idx),
            pl.BlockSpec((1, D_MODEL), const),
            pl.BlockSpec((None, D_MODEL, OFF_G), lambda i: (layer, 0, 0)),
            pl.BlockSpec((1, QK_WIDTH), const),
            pl.BlockSpec((tm, LANES), lambda i: (i % tps, 0)),
            pl.BlockSpec((tm, LANES), lambda i: (i % tps, 0)),
            pl.BlockSpec((FOURIER_WIDTH, 2 * FOURIER_WIDTH), const),
        ],
        out_specs=(
            pl.BlockSpec((tm, CONV_WIDTH), lambda i: (i, 0)),
            pl.BlockSpec((tm, CONV_WIDTH), lambda i: (i, 0)),
            fab_spec,
            pl.BlockSpec((None, ATTN_WIDTH, tm), lambda i: (i // tps, 0, i % tps)),
            pl.BlockSpec((tm, KV_WIDTH), lambda i: (i, 0)),
            pl.BlockSpec((None, N_KV_HEADS * V_ROWS, tm), lambda i: (i // tps, 0, i % tps)),
        ),
        args=(x2, mod, g1, w1, gqk, cos_t, sin_t, w64),
        scratch=scratch,
        temporaries=[((sub, OFF_G), F32), ((sub, D_MODEL), F32), ((QK_WIDTH + KV_WIDTH, sub), F32)] * 2,
    )


def _fft_kernel(z_ref, cr_ref, sr_ref, twc_ref, tws_ref, cc_ref, sc_ref, y_ref, g_scr, y_scr):
    w = FOURIER_WIDTH
    rows = z_ref.shape[1]
    pitch = _strided_pitch(rows)
    for n2 in range(GRID_W):
        zb = z_ref[n2]
        p = _dot(cr_ref[...], zb)
        q = _dot(sr_ref[...], zb)
        gr = p[:, :w] + q[:, w:]
        gi = p[:, w:] - q[:, :w]
        tc = twc_ref[n2]
        ts = tws_ref[n2]
        g = jnp.concatenate([gr * tc + gi * ts, gi * tc - gr * ts], axis=-1)
        _tiled_store(g_scr, slice(n2 * pitch, n2 * pitch + rows), g)
    for k1 in range(rows):
        slab = _tiled_load(g_scr, pl.ds(k1, GRID_W, stride=pitch)).astype(BF16)
        y = _dot(cc_ref[...], slab[:, :w]) + _dot(sc_ref[...], slab[:, w:])
        _tiled_store(y_scr, pl.ds(k1, GRID_W, stride=pitch), y)
    for k2 in range(GRID_W):
        y_ref[k2 * rows:(k2 + 1) * rows, :] = _tiled_load(y_scr, slice(k2 * pitch, k2 * pitch + rows)).astype(BF16)


def _fft_call(zt, c_r, s_r, twc, tws, c_c, s_c, *, batch, rows):
    seq = rows * GRID_W
    const2 = lambda b: (0, 0)
    const3 = lambda b: (0, 0, 0)
    pitch_rows = GRID_W * _strided_pitch(rows)
    return _pallas(
        _fft_kernel,
        name="fft",
        out_shape=jax.ShapeDtypeStruct((batch * seq, FOURIER_WIDTH), BF16),
        grid=(batch,),
        in_specs=[
            pl.BlockSpec((None, GRID_W, rows, 2 * FOURIER_WIDTH), lambda b: (b, 0, 0, 0)),
            pl.BlockSpec((rows, rows), const2),
            pl.BlockSpec((rows, rows), const2),
            pl.BlockSpec((GRID_W, rows, 1), const3),
            pl.BlockSpec((GRID_W, rows, 1), const3),
            pl.BlockSpec((GRID_W, GRID_W), const2),
            pl.BlockSpec((GRID_W, GRID_W), const2),
        ],
        out_specs=pl.BlockSpec((seq, FOURIER_WIDTH), lambda b: (b, 0)),
        args=(zt, c_r, s_r, twc, tws, c_c, s_c),
        scratch=[pltpu.VMEM((2 * FOURIER_WIDTH // LANES, pitch_rows, LANES), F32),
                 pltpu.VMEM((FOURIER_WIDTH // LANES, pitch_rows, LANES), F32)],
        temporaries=[((rows, 2 * FOURIER_WIDTH), F32)] * 8,
    )


def _dft_tables(n):
    idx = jnp.arange(n, dtype=jnp.int32)
    ang = (2.0 * jnp.pi / n) * ((idx[:, None] * idx[None, :]) % n).astype(F32)
    return jnp.cos(ang), jnp.sin(ang)


def _dft_dense_kernel(z_ref, c_ref, s_ref, y_ref):
    w = FOURIER_WIDTH
    y_ref[...] = (_dot(c_ref[...], z_ref[:, :w]) + _dot(s_ref[...], z_ref[:, w:])).astype(BF16)


def _dft_dense_call(fab, c_l, s_l, *, batch, seq):
    return _pallas(
        _dft_dense_kernel,
        name="dft_dense",
        out_shape=jax.ShapeDtypeStruct((batch * seq, FOURIER_WIDTH), BF16),
        grid=(batch,),
        in_specs=[
            pl.BlockSpec((seq, 2 * FOURIER_WIDTH), lambda b: (b, 0)),
            pl.BlockSpec((seq, seq), lambda b: (0, 0)),
            pl.BlockSpec((seq, seq), lambda b: (0, 0)),
        ],
        out_specs=pl.BlockSpec((seq, FOURIER_WIDTH), lambda b: (b, 0)),
        args=(fab, c_l, s_l),
        temporaries=[((seq, FOURIER_WIDTH), F32)] * 2,
    )


def _fourier_mix(fab, *, batch, seq):
    rows = seq // GRID_W
    scale = (seq * FOURIER_GROUP_DIM) ** -0.5
    if rows < BF16_SUBLANES:
        c_l, s_l = _dft_tables(seq)
        return _dft_dense_call(fab, (c_l * scale).astype(BF16), (s_l * scale).astype(BF16), batch=batch, seq=seq)
    c_r, s_r = _dft_tables(rows)
    c_c, s_c = _dft_tables(GRID_W)
    n2 = jnp.arange(GRID_W, dtype=jnp.int32)
    k1 = jnp.arange(rows, dtype=jnp.int32)
    tw_ang = (2.0 * jnp.pi / seq) * (n2[:, None] * k1[None, :]).astype(F32)[:, :, None]
    return _fft_call(fab, c_r.astype(BF16), s_r.astype(BF16), jnp.cos(tw_ang), jnp.sin(tw_ang),
                     (c_c * scale).astype(BF16), (s_c * scale).astype(BF16), batch=batch, rows=rows)


def _attn_kernel(*refs, chunks, tq, cast_cols):
    n_src = len(chunks)
    n_cast = len(cast_cols)
    qt_ref = refs[0]
    kv_refs = refs[1:1 + 2 * n_src]
    cast_in = refs[1 + 2 * n_src:1 + 2 * n_src + n_cast]
    o_ref = refs[1 + 2 * n_src + n_cast]
    cast_out = refs[2 + 2 * n_src + n_cast:2 + 2 * n_src + 2 * n_cast]
    s_scr = refs[-1]
    for src, dst, (c0, c1) in zip(cast_in, cast_out, cast_cols):
        dst[...] = src[:, c0:c1].astype(BF16)
    g = pl.program_id(1)
    row = lax.broadcasted_iota(jnp.int32, (KV_WIDTH, tq), 0)
    own_group = (row >= g * HEAD_DIM) & (row < (g + 1) * HEAD_DIM)
    steps = [(src, c) for src, (n_chunks, _) in enumerate(chunks) for c in range(n_chunks)]

    def k_chunk(src, c):
        tk = chunks[src][1]
        return kv_refs[2 * src][c * tk:(c + 1) * tk, :]

    def v_chunk(src, c):
        tk = chunks[src][1]
        return kv_refs[2 * src + 1][:, c * tk:(c + 1) * tk]

    def scores(qz_h, h, kc, slot):
        n = kc.shape[0]
        s = _dot(kc, qz_h)
        s_scr[slot, h, 0:n, :] = s
        part = jnp.max(s.reshape(n // HEAD_DIM, HEAD_DIM, tq), axis=0)
        return jnp.max(part, axis=0, keepdims=True)

    def accumulate(h, carry_h, chunk_max, vc, slot):
        n = vc.shape[1]
        m, acc = carry_h
        m_new = jnp.maximum(m, chunk_max)
        p = jnp.exp2(s_scr[slot, h, 0:n, :] - m_new).astype(BF16)
        return m_new, jnp.exp2(m - m_new) * acc + _dot(vc, p)

    slot = 0
    for q0 in range(0, qt_ref.shape[1], tq):
        qt = qt_ref[:, q0:q0 + tq]
        qz = [jnp.where(own_group, jnp.concatenate([qt[h * HEAD_DIM:(h + 1) * HEAD_DIM, :]] * N_KV_HEADS, axis=0),
                        jnp.zeros((), BF16)) for h in range(GQA_GROUP)]
        carry = [(jnp.full((1, tq), -jnp.inf, F32), jnp.zeros((V_ROWS, tq), F32)) for _ in range(GQA_GROUP)]
        first_chunk = k_chunk(*steps[0])
        maxima = [scores(qz[h], h, first_chunk, slot) for h in range(GQA_GROUP)]
        for t, cur in enumerate(steps):
            vc = v_chunk(*cur)
            kc = k_chunk(*steps[t + 1]) if t + 1 < len(steps) else None
            next_maxima = []
            for h in range(GQA_GROUP):
                if kc is not None:
                    next_maxima.append(scores(qz[h], h, kc, 1 - slot))
                carry[h] = accumulate(h, carry[h], maxima[h], vc, slot)
            maxima = next_maxima
            slot = 1 - slot
        outs = [acc[:HEAD_DIM, :] / acc[HEAD_DIM:HEAD_DIM + 1, :] for _, acc in carry]
        o_ref[q0:q0 + tq, :] = jnp.concatenate(outs, axis=0).T.astype(BF16)


def _attn_call(qt, sources, *, batch, seq, tq, tk, q_tiles, cast=(), layer=0):
    bq = tq * q_tiles
    nq = seq // bq
    gw = GQA_GROUP * HEAD_DIM
    n_steps = batch * N_KV_HEADS * nq
    step = lambda b, g, i: (b * N_KV_HEADS + g) * nq + i
    in_specs = [pl.BlockSpec((None, gw, bq), lambda b, g, i: (b, g, i))]
    args = [qt]
    chunks = []
    for k, vt in sources:
        ln = k.shape[1]
        ck = min(tk, ln)
        chunks.append((ln // ck, ck))
        in_specs.append(pl.BlockSpec((None, ln, KV_WIDTH), lambda b, g, i: (b, 0, 0)))
        in_specs.append(pl.BlockSpec((None, V_ROWS, ln), lambda b, g, i: (b, g, 0)))
        args += [k, vt]
    out_shape = [jax.ShapeDtypeStruct((batch * seq, ATTN_WIDTH), BF16)]
    out_specs = [pl.BlockSpec((bq, gw), lambda b, g, i: (b * nq + i, g))]
    for w, c0, c1 in cast:
        rows = w.shape[1] // n_steps
        assert rows * n_steps == w.shape[1] and rows % BF16_SUBLANES == 0, w.shape
        in_specs.append(pl.BlockSpec((None, rows, w.shape[2]), lambda b, g, i: (layer, step(b, g, i), 0)))
        args.append(w)
        out_shape.append(jax.ShapeDtypeStruct((w.shape[1], c1 - c0), BF16))
        out_specs.append(pl.BlockSpec((rows, c1 - c0), lambda b, g, i: (step(b, g, i), 0)))
    max_chunk = max(ck for _, ck in chunks)
    return _pallas(
        functools.partial(_attn_kernel, chunks=tuple(chunks), tq=tq, cast_cols=tuple((c0, c1) for _, c0, c1 in cast)),
        name="attention",
        out_shape=tuple(out_shape),
        grid=(batch, N_KV_HEADS, nq),
        in_specs=in_specs,
        out_specs=tuple(out_specs),
        args=args,
        scratch=[pltpu.VMEM((2, GQA_GROUP, max_chunk, tq), F32)],
        temporaries=[((max_chunk, tq), F32), ((max_chunk, tq), BF16), ((V_ROWS, tq), F32)] * GQA_GROUP,
    )


def _merge_mlp_kernel(x_ref, mod_ref, g1_ref, g2_ref, wg_ref, bz_ref, u_ref, up_ref, un_ref, wconv_ref,
                      yf_ref, ya_ref, wc_ref, wf_ref, wa_ref, wo_ref, w1_ref, w2_ref, o_ref, *, tps, sub, ff_chunk):
    tm = x_ref.shape[0]

    pos = pl.program_id(0) % tps
    u = u_ref[...].astype(F32)
    prev_row = jnp.where(pos == 0, 0.0, up_ref[...].astype(F32)[BF16_SUBLANES - 1:BF16_SUBLANES, :])
    next_row = jnp.where(pos == tps - 1, 0.0, un_ref[...].astype(F32)[0:1, :])
    row = lax.broadcasted_iota(jnp.int32, u.shape, 0)
    u_m1 = jnp.where(row == 0, prev_row, pltpu.roll(u, 1, 0))
    u_p1 = jnp.where(row == tm - 1, next_row, pltpu.roll(u, tm - 1, 0))
    conv = wconv_ref[0:1, :] * u_m1 + wconv_ref[1:2, :] * u + wconv_ref[2:3, :] * u_p1
    y_conv = (bz_ref[...].astype(F32) * conv).astype(BF16)

    d = D_MODEL
    for r0 in range(0, tm, sub):
        rows = slice(r0, r0 + sub)
        x = x_ref[rows, :]
        h = _modulate(x, g1_ref[...], mod_ref[0:1, :], mod_ref[1:2, :]).astype(BF16)
        m = _sigmoid(_dot(h, wg_ref[:, 0:d])) * _dot(y_conv[rows, :], wc_ref[...])
        m = m + _sigmoid(_dot(h, wg_ref[:, d:2 * d])) * _dot(yf_ref[rows, :], wf_ref[...])
        m = m + _sigmoid(_dot(h, wg_ref[:, 2 * d:3 * d])) * _dot(ya_ref[rows, :], wa_ref[...])
        x1 = x + mod_ref[2:3, :] * _dot(m.astype(BF16), wo_ref[...])

        h2 = _modulate(x1, g2_ref[...], mod_ref[3:4, :], mod_ref[4:5, :]).astype(BF16)
        acc = jnp.zeros(x1.shape, F32)
        for c in range(D_FF // ff_chunk):
            a = jnp.maximum(_dot(h2, w1_ref[:, c * ff_chunk:(c + 1) * ff_chunk]), 0.0)
            acc = acc + _dot((a * a).astype(BF16), w2_ref[c * ff_chunk:(c + 1) * ff_chunk, :])
        o_ref[rows, :] = x1 + mod_ref[5:6, :] * acc


def _merge_mlp_call(x2, mod, g1, g2, wg, bz, u, wconv, yf, ya, wc, wf, wa, wo, w1, w2, *,
                    layer, batch, seq, tm, shared_mod):
    n = batch * seq
    tps = seq // tm
    hb = tm // BF16_SUBLANES
    n_hb = n // BF16_SUBLANES
    mod_idx = (lambda i: (0, 0, 0)) if shared_mod else (lambda i: (i // tps, 0, 0))
    const = lambda i: (0, 0)
    tile = lambda w: pl.BlockSpec((tm, w), lambda i: (i, 0))
    def weight(w):
        if w.ndim == 2:
            return pl.BlockSpec(w.shape, const, pipeline_mode=pl.Buffered(1))
        return pl.BlockSpec((None,) + w.shape[1:], lambda i: (layer, 0, 0), pipeline_mode=pl.Buffered(1))
    sub = min(tm, MERGE_SUB)
    return _pallas(
        functools.partial(_merge_mlp_kernel, tps=tps, sub=sub, ff_chunk=FF_CHUNK),
        name="merge_mlp",
        out_shape=jax.ShapeDtypeStruct((n, D_MODEL), F32),
        grid=(n // tm,),
        in_specs=[
            tile(D_MODEL),
            pl.BlockSpec((None, N_MOD, D_MODEL), mod_idx),
            pl.BlockSpec((1, D_MODEL), const),
            pl.BlockSpec((1, D_MODEL), const),
            weight(wg),
            tile(CONV_WIDTH),
            tile(CONV_WIDTH),
            pl.BlockSpec((BF16_SUBLANES, CONV_WIDTH), lambda i: (jnp.maximum(i * hb - 1, 0), 0)),
            pl.BlockSpec((BF16_SUBLANES, CONV_WIDTH), lambda i: (jnp.minimum((i + 1) * hb, n_hb - 1), 0)),
            pl.BlockSpec((CONV_K, CONV_WIDTH), const),
            tile(FOURIER_WIDTH),
            tile(ATTN_WIDTH),
            weight(wc),
            weight(wf),
            weight(wa),
            weight(wo),
            weight(w1),
            weight(w2),
        ],
        out_specs=tile(D_MODEL),
        args=(x2, mod, g1, g2, wg, bz, u, u, u, wconv, yf, ya, wc, wf, wa, wo, w1, w2),
        temporaries=[((tm, CONV_WIDTH), F32)] * 4
        + ([((sub, D_MODEL), F32)] * 8 + [((sub, FF_CHUNK), F32), ((sub, FF_CHUNK), BF16)]) * (tm // sub),
    )


def _rope_tables(rows):
    n_freq = ROPE_HALF // 2
    inv = ROPE_THETA ** (-jnp.arange(n_freq, dtype=F32) / n_freq)
    row_ang = jnp.arange(rows, dtype=F32)[:, None] * inv
    col_ang = jnp.arange(GRID_W, dtype=F32)[:, None] * inv
    expand = lambda r, c: jnp.concatenate([jnp.repeat(r, GRID_W, axis=0), jnp.tile(c, (rows, 1))], axis=-1)
    cos = expand(jnp.cos(row_ang), jnp.cos(col_ang))
    sin = expand(jnp.sin(row_ang), jnp.sin(col_ang))
    cos_h = jnp.concatenate([cos, cos], axis=-1)
    sin_h = jnp.concatenate([-sin, sin], axis=-1)
    reps = LANES // HEAD_DIM
    return jnp.tile(cos_h, (1, reps)), jnp.tile(sin_h, (1, reps))


def _channel_dft_matrix():
    c, s = _dft_tables(FOURIER_GROUP_DIM)
    eye = jnp.eye(FOURIER_GROUPS, dtype=F32)
    return jnp.concatenate([jnp.kron(eye, c), jnp.kron(eye, -s)], axis=1).astype(BF16)


def kernel(x, c, ctx, c_ctx, w_mod, b_mod, g_norm1, g_norm2, w_in, w_conv, g_q, g_k,
           w_conv_out, w_four_out, w_attn_out, w_o, w_ff1, w_ff2):
    batch, seq, d = x.shape
    ctx_len = ctx.shape[1]
    depth = w_mod.shape[0]
    rows = seq // GRID_W

    cc = jnp.zeros((MOD_ROWS, d), F32).at[:batch].set(c).at[batch].set(c_ctx)
    mods = _mod_call(cc, w_mod, b_mod)

    cos_t, sin_t = _rope_tables(rows)
    cos_c = jnp.ones((ctx_len, LANES), F32)
    sin_c = jnp.zeros((ctx_len, LANES), F32)
    w64 = _channel_dft_matrix()
    w1, wc, wf, wa, wo = _cast_weights(w_in, w_conv_out, w_four_out, w_attn_out, w_o)
    big_weights = ((w_in, OFF_G, w_in.shape[2]), (w_ff1, 0, w_ff1.shape[2]), (w_ff2, 0, w_ff2.shape[2]))

    xs = x.reshape(batch * seq, d)
    cs = ctx.reshape(batch * ctx_len, d)
    for l in range(depth):
        mod_x = mods[l, :batch].reshape(batch, N_MOD, d)
        mod_c = mods[l, batch:batch + 1].reshape(1, N_MOD, d)
        g1 = g_norm1[l].reshape(1, d)
        g2 = g_norm2[l].reshape(1, d)
        gqk = jnp.concatenate([jnp.tile(g_q[l], N_Q_HEADS), jnp.tile(g_k[l], N_KV_HEADS)]).reshape(1, QK_WIDTH)
        last = l == depth - 1

        bz_c, u_c, fab_c, qt_c, k_c, vt_c = _in_proj_call(
            cs, mod_c, g1, w1, gqk, cos_c, sin_c, w64,
            layer=l, batch=batch, seq=ctx_len, tm=ctx_len, shared_mod=True)
        k_c3 = k_c.reshape(batch, ctx_len, KV_WIDTH)

        bz, u, fab, qt, k, vt = _in_proj_call(
            xs, mod_x, g1, w1, gqk, cos_t, sin_t, w64,
            layer=l, batch=batch, seq=seq, tm=IN_PROJ_TM, shared_mod=False)
        ya, wg, wf1, wf2 = _attn_call(qt, [(k.reshape(batch, seq, KV_WIDTH), vt), (k_c3, vt_c)],
                                      batch=batch, seq=seq, tq=ATTN_TQ, tk=ATTN_TK, q_tiles=ATTN_Q_TILES,
                                      cast=big_weights, layer=l)
        yf = _fourier_mix(fab, batch=batch, seq=seq)
        xs = _merge_mlp_call(xs, mod_x, g1, g2, wg, bz, u, w_conv[l], yf, ya, wc, wf, wa, wo, wf1, wf2,
                             layer=l, batch=batch, seq=seq, tm=MERGE_TM, shared_mod=False)

        if not last:
            (ya_c,) = _attn_call(qt_c, [(k_c3, vt_c)], batch=batch, seq=ctx_len, tq=ATTN_TQ, tk=ATTN_TK,
                                 q_tiles=ctx_len // ATTN_TQ)
            yf_c = _fourier_mix(fab_c, batch=batch, seq=ctx_len)
            cs = _merge_mlp_call(cs, mod_c, g1, g2, wg, bz_c, u_c, w_conv[l], yf_c, ya_c, wc, wf, wa, wo, wf1, wf2,
                                 layer=l, batch=batch, seq=ctx_len, tm=ctx_len, shared_mod=True)
    return xs.reshape(batch, seq, d)
```

```python
import functools
import math

import jax
import jax.numpy as jnp
from jax import lax
from jax.experimental import pallas as pl
from jax.experimental.pallas import tpu as pltpu

D_MODEL = 1024
GRID_W = 64
HEAD_DIM = 64
N_Q_HEADS = 8
N_KV_HEADS = 2
GQA_GROUP = N_Q_HEADS // N_KV_HEADS
ATTN_WIDTH = N_Q_HEADS * HEAD_DIM
KV_WIDTH = N_KV_HEADS * HEAD_DIM
CONV_WIDTH = 256
CONV_K = 3
FOURIER_GROUPS = 4
FOURIER_GROUP_DIM = 64
FOURIER_WIDTH = FOURIER_GROUPS * FOURIER_GROUP_DIM
N_BRANCHES = 3
D_FF = 4 * D_MODEL
ROPE_THETA = 10000.0
ROPE_HALF = HEAD_DIM // 2
EPS = 1e-6
N_MOD = 6

OFF_B = 0
OFF_C = OFF_B + CONV_WIDTH
OFF_X = OFF_C + CONV_WIDTH
OFF_F = OFF_X + CONV_WIDTH
OFF_Q = OFF_F + FOURIER_WIDTH
OFF_K = OFF_Q + ATTN_WIDTH
OFF_V = OFF_K + KV_WIDTH
OFF_G = OFF_V + KV_WIDTH

LANES = 128
SUBLANES = 8
BF16_SUBLANES = 16
QK_WIDTH = ATTN_WIDTH + KV_WIDTH
V_ROWS = HEAD_DIM + BF16_SUBLANES
Q_SCALE = HEAD_DIM ** -0.5 * 1.4426950408889634
ATTN_TQ = 256
ATTN_TK = 512
ATTN_Q_TILES = 2
MOD_TN = 1024
FF_CHUNK = 1024
CAST_STEPS = 8
IN_PROJ_TM = 1024
MERGE_TM = 512
IN_PROJ_SUB = 256
MERGE_SUB = 256
MOD_ROWS = 8

BF16 = jnp.bfloat16
F32 = jnp.float32


def _dot(a, b):
    return jnp.dot(a, b, preferred_element_type=F32)


def _tile_bytes(shape, dtype):
    itemsize = jnp.dtype(dtype).itemsize
    dims = [1 if d is None else d for d in shape]
    sublanes = SUBLANES * (4 // itemsize)
    dims[-1] = -(-dims[-1] // LANES) * LANES
    if len(dims) > 1:
        dims[-2] = -(-dims[-2] // sublanes) * sublanes
    return math.prod(dims) * itemsize


def _pallas(kernel, *, name, grid, in_specs, out_specs, out_shape, args, scratch=(), temporaries=()):
    outs = out_shape if isinstance(out_shape, tuple) else (out_shape,)
    ospecs = out_specs if isinstance(out_specs, tuple) else (out_specs,)
    total = 0
    for spec, dtype in [(s, a.dtype) for s, a in zip(in_specs, args)] + [(s, o.dtype) for s, o in zip(ospecs, outs)]:
        buffers = spec.pipeline_mode.buffer_count if spec.pipeline_mode is not None else 2
        total += buffers * _tile_bytes(spec.block_shape, dtype)
    total += sum(_tile_bytes(s.shape, s.dtype) for s in scratch)
    total += sum(_tile_bytes(shape, dtype) for shape, dtype in temporaries)
    return pl.pallas_call(
        kernel, out_shape=out_shape, grid=grid, in_specs=in_specs, out_specs=out_specs, scratch_shapes=list(scratch),
        compiler_params=pltpu.CompilerParams(dimension_semantics=("arbitrary",) * len(grid), vmem_limit_bytes=total),
        name=name,
    )(*args)


def _modulate(x, g, shift, scale):
    ms = jnp.mean(x * x, axis=-1, keepdims=True)
    y = x * lax.rsqrt(ms + EPS)
    return (y * g) * (1.0 + scale) + shift


def _sigmoid(x):
    return 1.0 / (1.0 + jnp.exp(-x))


def _strided_pitch(n):
    tiles = -(-n // SUBLANES)
    return SUBLANES * (tiles if tiles % 2 else tiles + 1)


def _tiled_store(scr, rows, value):
    for j in range(scr.shape[0]):
        scr[j, rows, :] = value[:, j * LANES:(j + 1) * LANES]


def _tiled_load(scr, rows):
    return jnp.concatenate([scr[j, rows, :] for j in range(scr.shape[0])], axis=-1)


def _mod_kernel(c_ref, w_ref, b_ref, o_ref):
    c = c_ref[...]
    s = (c * _sigmoid(c)).astype(BF16)
    o_ref[...] = _dot(s, w_ref[...].astype(BF16)) + b_ref[...]


def _mod_call(cc, w_mod, b_mod):
    depth = w_mod.shape[0]
    tn = MOD_TN
    return _pallas(
        _mod_kernel,
        name="mod",
        out_shape=jax.ShapeDtypeStruct((depth, MOD_ROWS, N_MOD * D_MODEL), F32),
        grid=(depth, N_MOD * D_MODEL // tn),
        in_specs=[
            pl.BlockSpec((MOD_ROWS, D_MODEL), lambda l, j: (0, 0)),
            pl.BlockSpec((None, D_MODEL, tn), lambda l, j: (l, 0, j)),
            pl.BlockSpec((None, 1, tn), lambda l, j: (l, 0, j)),
        ],
        out_specs=pl.BlockSpec((None, MOD_ROWS, tn), lambda l, j: (l, 0, j)),
        args=(cc, w_mod, b_mod.reshape(depth, 1, N_MOD * D_MODEL)),
        temporaries=[((D_MODEL, tn), BF16)],
    )


def _cast_kernel(*refs):
    n = len(refs) // 2
    for src, dst in zip(refs[:n], refs[n:]):
        dst[...] = src[...].astype(BF16)


def _cast_weights(w_in, w_conv_out, w_four_out, w_attn_out, w_o):
    depth = w_in.shape[0]
    ins = (w_in, w_conv_out, w_four_out, w_attn_out, w_o)
    cols = (OFF_G,) + tuple(w.shape[2] for w in ins[1:])
    spec = lambda w, c: pl.BlockSpec((None, w.shape[1] // CAST_STEPS, c), lambda l, i: (l, i, 0))
    return _pallas(
        _cast_kernel,
        name="cast_weights",
        out_shape=tuple(jax.ShapeDtypeStruct((depth, w.shape[1], c), BF16) for w, c in zip(ins, cols)),
        grid=(depth, CAST_STEPS),
        in_specs=[spec(w, c) for w, c in zip(ins, cols)],
        out_specs=tuple(spec(w, c) for w, c in zip(ins, cols)),
        args=ins,
    )


def _in_proj_kernel(x_ref, mod_ref, g1_ref, w_ref, gqk_ref, cos_ref, sin_ref, w64_ref,
                    bz_ref, u_ref, fab_ref, qt_ref, k_ref, vt_ref, *fab_scr, sub):
    tm = x_ref.shape[0]
    pitch = _strided_pitch(GRID_W)
    for r0 in range(0, tm, sub):
        rows = slice(r0, r0 + sub)
        h = _modulate(x_ref[rows, :], g1_ref[...], mod_ref[0:1, :], mod_ref[1:2, :]).astype(BF16)
        z = _dot(h, w_ref[...])
        bz_ref[rows, :] = z[:, OFF_B:OFF_C].astype(BF16)
        u_ref[rows, :] = (z[:, OFF_C:OFF_X] * z[:, OFF_X:OFF_F]).astype(BF16)
        fab = _dot(z[:, OFF_F:OFF_Q].astype(BF16), w64_ref[...])
        if fab_scr:
            for a in range(sub // GRID_W):
                start = (r0 // GRID_W + a) * pitch
                _tiled_store(fab_scr[0], slice(start, start + GRID_W), fab[a * GRID_W:(a + 1) * GRID_W, :])
        else:
            fab_ref[rows, :] = fab.astype(BF16)

        lane = lax.broadcasted_iota(jnp.int32, (sub, LANES), 1)
        head_lo = lane < HEAD_DIM
        half_lo = (lane % HEAD_DIM) < ROPE_HALF
        cos = cos_ref[rows, :]
        sin = sin_ref[rows, :]
        for j in range(QK_WIDTH // LANES):
            zj = z[:, OFF_Q + j * LANES:OFF_Q + (j + 1) * LANES]
            sq = zj * zj
            s_lo = jnp.sum(jnp.where(head_lo, sq, 0.0), axis=-1, keepdims=True)
            s_hi = jnp.sum(jnp.where(head_lo, 0.0, sq), axis=-1, keepdims=True)
            r = jnp.where(head_lo, lax.rsqrt(s_lo / HEAD_DIM + EPS), lax.rsqrt(s_hi / HEAD_DIM + EPS))
            n = (zj * r) * gqk_ref[:, j * LANES:(j + 1) * LANES]
            partner = jnp.where(half_lo, pltpu.roll(n, LANES - ROPE_HALF, 1), pltpu.roll(n, ROPE_HALF, 1))
            rot = n * cos + partner * sin
            if j < ATTN_WIDTH // LANES:
                qt_ref[j * LANES:(j + 1) * LANES, rows] = (rot * Q_SCALE).T.astype(BF16)
            else:
                k_ref[rows, :] = rot.astype(BF16)
        vt = z[:, OFF_V:OFF_G].T.astype(BF16)
        for g in range(N_KV_HEADS):
            vt_ref[g * V_ROWS:g * V_ROWS + HEAD_DIM, rows] = vt[g * HEAD_DIM:(g + 1) * HEAD_DIM, :]
            vt_ref[g * V_ROWS + HEAD_DIM:(g + 1) * V_ROWS, rows] = jnp.ones((V_ROWS - HEAD_DIM, sub), BF16)
    if fab_scr:
        for n2 in range(GRID_W):
            fab_ref[n2] = _tiled_load(fab_scr[0], pl.ds(n2, tm // GRID_W, stride=pitch)).astype(BF16)


def _in_proj_call(x2, mod, g1, w1, gqk, cos_t, sin_t, w64, *, layer, batch, seq, tm, shared_mod):
    n = batch * seq
    tps = seq // tm
    mod_idx = (lambda i: (0, 0, 0)) if shared_mod else (lambda i: (i // tps, 0, 0))
    const = lambda i: (0, 0)
    tile_rows = tm // GRID_W
    column_major = tile_rows >= BF16_SUBLANES
    if column_major:
        fab_shape = jax.ShapeDtypeStruct((batch, GRID_W, seq // GRID_W, 2 * FOURIER_WIDTH), BF16)
        fab_spec = pl.BlockSpec((None, GRID_W, tile_rows, 2 * FOURIER_WIDTH), lambda i: (i // tps, 0, i % tps, 0))
        scratch = [pltpu.VMEM((2 * FOURIER_WIDTH // LANES, tile_rows * _strided_pitch(GRID_W), LANES), F32)]
    else:
        fab_shape = jax.ShapeDtypeStruct((n, 2 * FOURIER_WIDTH), BF16)
        fab_spec = pl.BlockSpec((tm, 2 * FOURIER_WIDTH), lambda i: (i, 0))
        scratch = []
    sub = min(tm, IN_PROJ_SUB)
    return _pallas(
        functools.partial(_in_proj_kernel, sub=sub),
        name="in_proj",
        out_shape=(
            jax.ShapeDtypeStruct((n, CONV_WIDTH), BF16),
            jax.ShapeDtypeStruct((n, CONV_WIDTH), BF16),
            fab_shape,
            jax.ShapeDtypeStruct((batch, ATTN_WIDTH, seq), BF16),
            jax.ShapeDtypeStruct((n, KV_WIDTH), BF16),
            jax.ShapeDtypeStruct((batch, N_KV_HEADS * V_ROWS, seq), BF16),
        ),
        grid=(n // tm,),
        in_specs=[
            pl.BlockSpec((tm, D_MODEL), lambda i: (i, 0)),
            pl.BlockSpec((None, N_MOD, D_MODEL), mod_idx),
            pl.BlockSpec((1, D_MODEL), const),
            pl.BlockSpec((None, D_MODEL, OFF_G), lambda i: (layer, 0, 0)),
            pl.BlockSpec((1, QK_WIDTH), const),
            pl.BlockSpec((tm, LANES), lambda i: (i % tps, 0)),
            pl.BlockSpec((tm, LANES), lambda i: (i % tps, 0)),
            pl.BlockSpec((FOURIER_WIDTH, 2 * FOURIER_WIDTH), const),
        ],
        out_specs=(
            pl.BlockSpec((tm, CONV_WIDTH), lambda i: (i, 0)),
            pl.BlockSpec((tm, CONV_WIDTH), lambda i: (i, 0)),
            fab_spec,
            pl.BlockSpec((None, ATTN_WIDTH, tm), lambda i: (i // tps, 0, i % tps)),
            pl.BlockSpec((tm, KV_WIDTH), lambda i: (i, 0)),
            pl.BlockSpec((None, N_KV_HEADS * V_ROWS, tm), lambda i: (i // tps, 0, i % tps)),
        ),
        args=(x2, mod, g1, w1, gqk, cos_t, sin_t, w64),
        scratch=scratch,
        temporaries=[((sub, OFF_G), F32), ((sub, D_MODEL), F32), ((QK_WIDTH + KV_WIDTH, sub), F32)] * 2,
    )


def _fft_kernel(z_ref, cr_ref, sr_ref, twc_ref, tws_ref, cc_ref, sc_ref, y_ref, g_scr, y_scr):
    w = FOURIER_WIDTH
    rows = z_ref.shape[1]
    pitch = _strided_pitch(rows)
    for n2 in range(GRID_W):
        zb = z_ref[n2]
        p = _dot(cr_ref[...], zb)
        q = _dot(sr_ref[...], zb)
        gr = p[:, :w] + q[:, w:]
        gi = p[:, w:] - q[:, :w]
        tc = twc_ref[n2]
        ts = tws_ref[n2]
        g = jnp.concatenate([gr * tc + gi * ts, gi * tc - gr * ts], axis=-1)
        _tiled_store(g_scr, slice(n2 * pitch, n2 * pitch + rows), g)
    for k1 in range(rows):
        slab = _tiled_load(g_scr, pl.ds(k1, GRID_W, stride=pitch)).astype(BF16)
        y = _dot(cc_ref[...], slab[:, :w]) + _dot(sc_ref[...], slab[:, w:])
        _tiled_store(y_scr, pl.ds(k1, GRID_W, stride=pitch), y)
    for k2 in range(GRID_W):
        y_ref[k2 * rows:(k2 + 1) * rows, :] = _tiled_load(y_scr, slice(k2 * pitch, k2 * pitch + rows)).astype(BF16)


def _fft_call(zt, c_r, s_r, twc, tws, c_c, s_c, *, batch, rows):
    seq = rows * GRID_W
    const2 = lambda b: (0, 0)
    const3 = lambda b: (0, 0, 0)
    pitch_rows = GRID_W * _strided_pitch(rows)
    return _pallas(
        _fft_kernel,
        name="fft",
        out_shape=jax.ShapeDtypeStruct((batch * seq, FOURIER_WIDTH), BF16),
        grid=(batch,),
        in_specs=[
            pl.BlockSpec((None, GRID_W, rows, 2 * FOURIER_WIDTH), lambda b: (b, 0, 0, 0)),
            pl.BlockSpec((rows, rows), const2),
            pl.BlockSpec((rows, rows), const2),
            pl.BlockSpec((GRID_W, rows, 1), const3),
            pl.BlockSpec((GRID_W, rows, 1), const3),
            pl.BlockSpec((GRID_W, GRID_W), const2),
            pl.BlockSpec((GRID_W, GRID_W), const2),
        ],
        out_specs=pl.BlockSpec((seq, FOURIER_WIDTH), lambda b: (b, 0)),
        args=(zt, c_r, s_r, twc, tws, c_c, s_c),
        scratch=[pltpu.VMEM((2 * FOURIER_WIDTH // LANES, pitch_rows, LANES), F32),
                 pltpu.VMEM((FOURIER_WIDTH // LANES, pitch_rows, LANES), F32)],
        temporaries=[((rows, 2 * FOURIER_WIDTH), F32)] * 8,
    )


def _dft_tables(n):
    idx = jnp.arange(n, dtype=jnp.int32)
    ang = (2.0 * jnp.pi / n) * ((idx[:, None] * idx[None, :]) % n).astype(F32)
    return jnp.cos(ang), jnp.sin(ang)


def _dft_dense_kernel(z_ref, c_ref, s_ref, y_ref):
    w = FOURIER_WIDTH
    y_ref[...] = (_dot(c_ref[...], z_ref[:, :w]) + _dot(s_ref[...], z_ref[:, w:])).astype(BF16)


def _dft_dense_call(fab, c_l, s_l, *, batch, seq):
    return _pallas(
        _dft_dense_kernel,
        name="dft_dense",
        out_shape=jax.ShapeDtypeStruct((batch * seq, FOURIER_WIDTH), BF16),
        grid=(batch,),
        in_specs=[
            pl.BlockSpec((seq, 2 * FOURIER_WIDTH), lambda b: (b, 0)),
            pl.BlockSpec((seq, seq), lambda b: (0, 0)),
            pl.BlockSpec((seq, seq), lambda b: (0, 0)),
        ],
        out_specs=pl.BlockSpec((seq, FOURIER_WIDTH), lambda b: (b, 0)),
        args=(fab, c_l, s_l),
        temporaries=[((seq, FOURIER_WIDTH), F32)] * 2,
    )


def _fourier_mix(fab, *, batch, seq):
    rows = seq // GRID_W
    scale = (seq * FOURIER_GROUP_DIM) ** -0.5
    if rows < BF16_SUBLANES:
        c_l, s_l = _dft_tables(seq)
        return _dft_dense_call(fab, (c_l * scale).astype(BF16), (s_l * scale).astype(BF16), batch=batch, seq=seq)
    c_r, s_r = _dft_tables(rows)
    c_c, s_c = _dft_tables(GRID_W)
    n2 = jnp.arange(GRID_W, dtype=jnp.int32)
    k1 = jnp.arange(rows, dtype=jnp.int32)
    tw_ang = (2.0 * jnp.pi / seq) * (n2[:, None] * k1[None, :]).astype(F32)[:, :, None]
    return _fft_call(fab, c_r.astype(BF16), s_r.astype(BF16), jnp.cos(tw_ang), jnp.sin(tw_ang),
                     (c_c * scale).astype(BF16), (s_c * scale).astype(BF16), batch=batch, rows=rows)


def _attn_kernel(*refs, chunks, tq, cast_cols):
    n_src = len(chunks)
    n_cast = len(cast_cols)
    qt_ref = refs[0]
    kv_refs = refs[1:1 + 2 * n_src]
    cast_in = refs[1 + 2 * n_src:1 + 2 * n_src + n_cast]
    o_ref = refs[1 + 2 * n_src + n_cast]
    cast_out = refs[2 + 2 * n_src + n_cast:2 + 2 * n_src + 2 * n_cast]
    s_scr = refs[-2:]
    row0 = pl.multiple_of(jnp.minimum(pl.program_id(2), 0), SUBLANES)
    rows = lambda n: pl.ds(row0, n)
    for src, dst, (c0, c1) in zip(cast_in, cast_out, cast_cols):
        dst[...] = src[:, c0:c1].astype(BF16)
    g = pl.program_id(1)
    row = lax.broadcasted_iota(jnp.int32, (KV_WIDTH, tq), 0)
    own_group = (row >= g * HEAD_DIM) & (row < (g + 1) * HEAD_DIM)
    steps = [(src, c) for src, (n_chunks, _) in enumerate(chunks) for c in range(n_chunks)]

    def k_chunk(src, c):
        tk = chunks[src][1]
        return kv_refs[2 * src][c * tk:(c + 1) * tk, :]

    def v_chunk(src, c):
        tk = chunks[src][1]
        return kv_refs[2 * src + 1][:, c * tk:(c + 1) * tk]

    def scores(qz_h, h, kc, slot):
        n = kc.shape[0]
        s = _dot(kc, qz_h)
        s_scr[slot][h, rows(n), :] = s
        part = jnp.max(s.reshape(n // HEAD_DIM, HEAD_DIM, tq), axis=0)
        return jnp.max(part, axis=0, keepdims=True)

    def accumulate(h, carry_h, chunk_max, vc, slot):
        n = vc.shape[1]
        m, acc = carry_h
        m_new = jnp.maximum(m, chunk_max)
        p = jnp.exp2(s_scr[slot][h, rows(n), :] - m_new).astype(BF16)
        return m_new, jnp.exp2(m - m_new) * acc + _dot(vc, p)

    slot = 0
    for q0 in range(0, qt_ref.shape[1], tq):
        qt = qt_ref[:, q0:q0 + tq]
        qz = [jnp.where(own_group, jnp.concatenate([qt[h * HEAD_DIM:(h + 1) * HEAD_DIM, :]] * N_KV_HEADS, axis=0),
                        jnp.zeros((), BF16)) for h in range(GQA_GROUP)]
        carry = [(jnp.full((1, tq), -jnp.inf, F32), jnp.zeros((V_ROWS, tq), F32)) for _ in range(GQA_GROUP)]
        first_chunk = k_chunk(*steps[0])
        maxima = [scores(qz[h], h, first_chunk, slot) for h in range(GQA_GROUP)]
        for t, cur in enumerate(steps):
            vc = v_chunk(*cur)
            kc = k_chunk(*steps[t + 1]) if t + 1 < len(steps) else None
            next_maxima = []
            for h in range(GQA_GROUP):
                if kc is not None:
                    next_maxima.append(scores(qz[h], h, kc, 1 - slot))
                carry[h] = accumulate(h, carry[h], maxima[h], vc, slot)
            maxima = next_maxima
            slot = 1 - slot
        outs = [acc[:HEAD_DIM, :] / acc[HEAD_DIM:HEAD_DIM + 1, :] for _, acc in carry]
        o_ref[q0:q0 + tq, :] = jnp.concatenate(outs, axis=0).T.astype(BF16)


def _attn_call(qt, sources, *, batch, seq, tq, tk, q_tiles, cast=(), layer=0):
    bq = tq * q_tiles
    nq = seq // bq
    gw = GQA_GROUP * HEAD_DIM
    n_steps = batch * N_KV_HEADS * nq
    step = lambda b, g, i: (b * N_KV_HEADS + g) * nq + i
    in_specs = [pl.BlockSpec((None, gw, bq), lambda b, g, i: (b, g, i))]
    args = [qt]
    chunks = []
    for k, vt in sources:
        ln = k.shape[1]
        ck = min(tk, ln)
        chunks.append((ln // ck, ck))
        in_specs.append(pl.BlockSpec((None, ln, KV_WIDTH), lambda b, g, i: (b, 0, 0)))
        in_specs.append(pl.BlockSpec((None, V_ROWS, ln), lambda b, g, i: (b, g, 0)))
        args += [k, vt]
    out_shape = [jax.ShapeDtypeStruct((batch * seq, ATTN_WIDTH), BF16)]
    out_specs = [pl.BlockSpec((bq, gw), lambda b, g, i: (b * nq + i, g))]
    for w, c0, c1 in cast:
        rows = w.shape[1] // n_steps
        assert rows * n_steps == w.shape[1] and rows % BF16_SUBLANES == 0, w.shape
        in_specs.append(pl.BlockSpec((None, rows, w.shape[2]), lambda b, g, i: (layer, step(b, g, i), 0)))
        args.append(w)
        out_shape.append(jax.ShapeDtypeStruct((w.shape[1], c1 - c0), BF16))
        out_specs.append(pl.BlockSpec((rows, c1 - c0), lambda b, g, i: (step(b, g, i), 0)))
    max_chunk = max(ck for _, ck in chunks)
    return _pallas(
        functools.partial(_attn_kernel, chunks=tuple(chunks), tq=tq, cast_cols=tuple((c0, c1) for _, c0, c1 in cast)),
        name="attention",
        out_shape=tuple(out_shape),
        grid=(batch, N_KV_HEADS, nq),
        in_specs=in_specs,
        out_specs=tuple(out_specs),
        args=args,
        scratch=[pltpu.VMEM((GQA_GROUP, max_chunk, tq), F32)] * 2,
        temporaries=[((max_chunk, tq), F32), ((max_chunk, tq), BF16), ((V_ROWS, tq), F32)] * GQA_GROUP,
    )


def _merge_mlp_kernel(x_ref, mod_ref, g1_ref, g2_ref, wg_ref, bz_ref, u_ref, up_ref, un_ref, wconv_ref,
                      yf_ref, ya_ref, wc_ref, wf_ref, wa_ref, wo_ref, w1_ref, w2_ref, o_ref, *, tps, sub, ff_chunk):
    tm = x_ref.shape[0]

    pos = pl.program_id(0) % tps
    u = u_ref[...].astype(F32)
    prev_row = jnp.where(pos == 0, 0.0, up_ref[...].astype(F32)[BF16_SUBLANES - 1:BF16_SUBLANES, :])
    next_row = jnp.where(pos == tps - 1, 0.0, un_ref[...].astype(F32)[0:1, :])
    row = lax.broadcasted_iota(jnp.int32, u.shape, 0)
    u_m1 = jnp.where(row == 0, prev_row, pltpu.roll(u, 1, 0))
    u_p1 = jnp.where(row == tm - 1, next_row, pltpu.roll(u, tm - 1, 0))
    conv = wconv_ref[0:1, :] * u_m1 + wconv_ref[1:2, :] * u + wconv_ref[2:3, :] * u_p1
    y_conv = (bz_ref[...].astype(F32) * conv).astype(BF16)

    d = D_MODEL
    for r0 in range(0, tm, sub):
        rows = slice(r0, r0 + sub)
        x = x_ref[rows, :]
        h = _modulate(x, g1_ref[...], mod_ref[0:1, :], mod_ref[1:2, :]).astype(BF16)
        m = _sigmoid(_dot(h, wg_ref[:, 0:d])) * _dot(y_conv[rows, :], wc_ref[...])
        m = m + _sigmoid(_dot(h, wg_ref[:, d:2 * d])) * _dot(yf_ref[rows, :], wf_ref[...])
        m = m + _sigmoid(_dot(h, wg_ref[:, 2 * d:3 * d])) * _dot(ya_ref[rows, :], wa_ref[...])
        x1 = x + mod_ref[2:3, :] * _dot(m.astype(BF16), wo_ref[...])

        h2 = _modulate(x1, g2_ref[...], mod_ref[3:4, :], mod_ref[4:5, :]).astype(BF16)
        acc = jnp.zeros(x1.shape, F32)
        for c in range(D_FF // ff_chunk):
            a = jnp.maximum(_dot(h2, w1_ref[:, c * ff_chunk:(c + 1) * ff_chunk]), 0.0)
            acc = acc + _dot((a * a).astype(BF16), w2_ref[c * ff_chunk:(c + 1) * ff_chunk, :])
        o_ref[rows, :] = x1 + mod_ref[5:6, :] * acc


def _merge_mlp_call(x2, mod, g1, g2, wg, bz, u, wconv, yf, ya, wc, wf, wa, wo, w1, w2, *,
                    layer, batch, seq, tm, shared_mod):
    n = batch * seq
    tps = seq // tm
    hb = tm // BF16_SUBLANES
    n_hb = n // BF16_SUBLANES
    mod_idx = (lambda i: (0, 0, 0)) if shared_mod else (lambda i: (i // tps, 0, 0))
    const = lambda i: (0, 0)
    tile = lambda w: pl.BlockSpec((tm, w), lambda i: (i, 0))
    def weight(w):
        if w.ndim == 2:
            return pl.BlockSpec(w.shape, const, pipeline_mode=pl.Buffered(1))
        return pl.BlockSpec((None,) + w.shape[1:], lambda i: (layer, 0, 0), pipeline_mode=pl.Buffered(1))
    sub = min(tm, MERGE_SUB)
    return _pallas(
        functools.partial(_merge_mlp_kernel, tps=tps, sub=sub, ff_chunk=FF_CHUNK),
        name="merge_mlp",
        out_shape=jax.ShapeDtypeStruct((n, D_MODEL), F32),
        grid=(n // tm,),
        in_specs=[
            tile(D_MODEL),
            pl.BlockSpec((None, N_MOD, D_MODEL), mod_idx),
            pl.BlockSpec((1, D_MODEL), const),
            pl.BlockSpec((1, D_MODEL), const),
            weight(wg),
            tile(CONV_WIDTH),
            tile(CONV_WIDTH),
            pl.BlockSpec((BF16_SUBLANES, CONV_WIDTH), lambda i: (jnp.maximum(i * hb - 1, 0), 0)),
            pl.BlockSpec((BF16_SUBLANES, CONV_WIDTH), lambda i: (jnp.minimum((i + 1) * hb, n_hb - 1), 0)),
            pl.BlockSpec((CONV_K, CONV_WIDTH), const),
            tile(FOURIER_WIDTH),
            tile(ATTN_WIDTH),
            weight(wc),
            weight(wf),
            weight(wa),
            weight(wo),
            weight(w1),
            weight(w2),
        ],
        out_specs=tile(D_MODEL),
        args=(x2, mod, g1, g2, wg, bz, u, u, u, wconv, yf, ya, wc, wf, wa, wo, w1, w2),
        temporaries=[((tm, CONV_WIDTH), F32)] * 4
        + ([((sub, D_MODEL), F32)] * 8 + [((sub, FF_CHUNK), F32), ((sub, FF_CHUNK), BF16)]) * (tm // sub),
    )


def _rope_tables(rows):
    n_freq = ROPE_HALF // 2
    inv = ROPE_THETA ** (-jnp.arange(n_freq, dtype=F32) / n_freq)
    row_ang = jnp.arange(rows, dtype=F32)[:, None] * inv
    col_ang = jnp.arange(GRID_W, dtype=F32)[:, None] * inv
    expand = lambda r, c: jnp.concatenate([jnp.repeat(r, GRID_W, axis=0), jnp.tile(c, (rows, 1))], axis=-1)
    cos = expand(jnp.cos(row_ang), jnp.cos(col_ang))
    sin = expand(jnp.sin(row_ang), jnp.sin(col_ang))
    cos_h = jnp.concatenate([cos, cos], axis=-1)
    sin_h = jnp.concatenate([-sin, sin], axis=-1)
    reps = LANES // HEAD_DIM
    return jnp.tile(cos_h, (1, reps)), jnp.tile(sin_h, (1, reps))


def _channel_dft_matrix():
    c, s = _dft_tables(FOURIER_GROUP_DIM)
    eye = jnp.eye(FOURIER_GROUPS, dtype=F32)
    return jnp.concatenate([jnp.kron(eye, c), jnp.kron(eye, -s)], axis=1).astype(BF16)


def kernel(x, c, ctx, c_ctx, w_mod, b_mod, g_norm1, g_norm2, w_in, w_conv, g_q, g_k,
           w_conv_out, w_four_out, w_attn_out, w_o, w_ff1, w_ff2):
    batch, seq, d = x.shape
    ctx_len = ctx.shape[1]
    depth = w_mod.shape[0]
    rows = seq // GRID_W

    cc = jnp.zeros((MOD_ROWS, d), F32).at[:batch].set(c).at[batch].set(c_ctx)
    mods = _mod_call(cc, w_mod, b_mod)

    cos_t, sin_t = _rope_tables(rows)
    cos_c = jnp.ones((ctx_len, LANES), F32)
    sin_c = jnp.zeros((ctx_len, LANES), F32)
    w64 = _channel_dft_matrix()
    w1, wc, wf, wa, wo = _cast_weights(w_in, w_conv_out, w_four_out, w_attn_out, w_o)
    big_weights = ((w_in, OFF_G, w_in.shape[2]), (w_ff1, 0, w_ff1.shape[2]), (w_ff2, 0, w_ff2.shape[2]))

    xs = x.reshape(batch * seq, d)
    cs = ctx.reshape(batch * ctx_len, d)
    for l in range(depth):
        mod_x = mods[l, :batch].reshape(batch, N_MOD, d)
        mod_c = mods[l, batch:batch + 1].reshape(1, N_MOD, d)
        g1 = g_norm1[l].reshape(1, d)
        g2 = g_norm2[l].reshape(1, d)
        gqk = jnp.concatenate([jnp.tile(g_q[l], N_Q_HEADS), jnp.tile(g_k[l], N_KV_HEADS)]).reshape(1, QK_WIDTH)
        last = l == depth - 1

        bz_c, u_c, fab_c, qt_c, k_c, vt_c = _in_proj_call(
            cs, mod_c, g1, w1, gqk, cos_c, sin_c, w64,
            layer=l, batch=batch, seq=ctx_len, tm=ctx_len, shared_mod=True)
        k_c3 = k_c.reshape(batch, ctx_len, KV_WIDTH)

        bz, u, fab, qt, k, vt = _in_proj_call(
            xs, mod_x, g1, w1, gqk, cos_t, sin_t, w64,
            layer=l, batch=batch, seq=seq, tm=IN_PROJ_TM, shared_mod=False)
        ya, wg, wf1, wf2 = _attn_call(qt, [(k.reshape(batch, seq, KV_WIDTH), vt), (k_c3, vt_c)],
                                      batch=batch, seq=seq, tq=ATTN_TQ, tk=ATTN_TK, q_tiles=ATTN_Q_TILES,
                                      cast=big_weights, layer=l)
        yf = _fourier_mix(fab, batch=batch, seq=seq)
        xs = _merge_mlp_call(xs, mod_x, g1, g2, wg, bz, u, w_conv[l], yf, ya, wc, wf, wa, wo, wf1, wf2,
                             layer=l, batch=batch, seq=seq, tm=MERGE_TM, shared_mod=False)

        if not last:
            (ya_c,) = _attn_call(qt_c, [(k_c3, vt_c)], batch=batch, seq=ctx_len, tq=ATTN_TQ, tk=ATTN_TK,
                                 q_tiles=ctx_len // ATTN_TQ)
            yf_c = _fourier_mix(fab_c, batch=batch, seq=ctx_len)
            cs = _merge_mlp_call(cs, mod_c, g1, g2, wg, bz_c, u_c, w_conv[l], yf_c, ya_c, wc, wf, wa, wo, wf1, wf2,
                                 layer=l, batch=batch, seq=ctx_len, tm=ctx_len, shared_mod=True)
    return xs.reshape(batch, seq, d)
```

```python
import functools
import math

import jax
import jax.numpy as jnp
from jax import lax
from jax.experimental import pallas as pl
from jax.experimental.pallas import tpu as pltpu

D_MODEL = 1024
GRID_W = 64
HEAD_DIM = 64
N_Q_HEADS = 8
N_KV_HEADS = 2
GQA_GROUP = N_Q_HEADS // N_KV_HEADS
ATTN_WIDTH = N_Q_HEADS * HEAD_DIM
KV_WIDTH = N_KV_HEADS * HEAD_DIM
CONV_WIDTH = 256
CONV_K = 3
FOURIER_GROUPS = 4
FOURIER_GROUP_DIM = 64
FOURIER_WIDTH = FOURIER_GROUPS * FOURIER_GROUP_DIM
N_BRANCHES = 3
D_FF = 4 * D_MODEL
ROPE_THETA = 10000.0
ROPE_HALF = HEAD_DIM // 2
EPS = 1e-6
N_MOD = 6

OFF_B = 0
OFF_C = OFF_B + CONV_WIDTH
OFF_X = OFF_C + CONV_WIDTH
OFF_F = OFF_X + CONV_WIDTH
OFF_Q = OFF_F + FOURIER_WIDTH
OFF_K = OFF_Q + ATTN_WIDTH
OFF_V = OFF_K + KV_WIDTH
OFF_G = OFF_V + KV_WIDTH

LANES = 128
SUBLANES = 8
BF16_SUBLANES = 16
QK_WIDTH = ATTN_WIDTH + KV_WIDTH
V_ROWS = HEAD_DIM + BF16_SUBLANES
Q_SCALE = HEAD_DIM ** -0.5 * 1.4426950408889634
ATTN_TQ = 256
ATTN_TK = 512
ATTN_Q_TILES = 2
MOD_TN = 1024
FF_CHUNK = 1024
CAST_STEPS = 8
IN_PROJ_TM = 1024
MERGE_TM = 512
IN_PROJ_SUB = 256
MERGE_SUB = 256
MOD_ROWS = 8

BF16 = jnp.bfloat16
F32 = jnp.float32


def _dot(a, b):
    return jnp.dot(a, b, preferred_element_type=F32)


def _tile_bytes(shape, dtype):
    itemsize = jnp.dtype(dtype).itemsize
    dims = [1 if d is None else d for d in shape]
    sublanes = SUBLANES * (4 // itemsize)
    dims[-1] = -(-dims[-1] // LANES) * LANES
    if len(dims) > 1:
        dims[-2] = -(-dims[-2] // sublanes) * sublanes
    return math.prod(dims) * itemsize


def _pallas(kernel, *, name, grid, in_specs, out_specs, out_shape, args, scratch=(), temporaries=()):
    outs = out_shape if isinstance(out_shape, tuple) else (out_shape,)
    ospecs = out_specs if isinstance(out_specs, tuple) else (out_specs,)
    total = 0
    for spec, dtype in [(s, a.dtype) for s, a in zip(in_specs, args)] + [(s, o.dtype) for s, o in zip(ospecs, outs)]:
        buffers = spec.pipeline_mode.buffer_count if spec.pipeline_mode is not None else 2
        total += buffers * _tile_bytes(spec.block_shape, dtype)
    total += sum(_tile_bytes(s.shape, s.dtype) for s in scratch)
    total += sum(_tile_bytes(shape, dtype) for shape, dtype in temporaries)
    return pl.pallas_call(
        kernel, out_shape=out_shape, grid=grid, in_specs=in_specs, out_specs=out_specs, scratch_shapes=list(scratch),
        compiler_params=pltpu.CompilerParams(dimension_semantics=("arbitrary",) * len(grid), vmem_limit_bytes=total),
        name=name,
    )(*args)


def _modulate(x, g, shift, scale):
    ms = jnp.mean(x * x, axis=-1, keepdims=True)
    y = x * lax.rsqrt(ms + EPS)
    return (y * g) * (1.0 + scale) + shift


def _sigmoid(x):
    return 1.0 / (1.0 + jnp.exp(-x))


def _strided_pitch(n):
    tiles = -(-n // SUBLANES)
    return SUBLANES * (tiles if tiles % 2 else tiles + 1)


def _tiled_store(scr, rows, value):
    for j in range(scr.shape[0]):
        scr[j, rows, :] = value[:, j * LANES:(j + 1) * LANES]


def _tiled_load(scr, rows):
    return jnp.concatenate([scr[j, rows, :] for j in range(scr.shape[0])], axis=-1)


def _mod_kernel(c_ref, w_ref, b_ref, o_ref):
    c = c_ref[...]
    s = (c * _sigmoid(c)).astype(BF16)
    o_ref[...] = _dot(s, w_ref[...].astype(BF16)) + b_ref[...]


def _mod_call(cc, w_mod, b_mod):
    depth = w_mod.shape[0]
    tn = MOD_TN
    return _pallas(
        _mod_kernel,
        name="mod",
        out_shape=jax.ShapeDtypeStruct((depth, MOD_ROWS, N_MOD * D_MODEL), F32),
        grid=(depth, N_MOD * D_MODEL // tn),
        in_specs=[
            pl.BlockSpec((MOD_ROWS, D_MODEL), lambda l, j: (0, 0)),
            pl.BlockSpec((None, D_MODEL, tn), lambda l, j: (l, 0, j)),
            pl.BlockSpec((None, 1, tn), lambda l, j: (l, 0, j)),
        ],
        out_specs=pl.BlockSpec((None, MOD_ROWS, tn), lambda l, j: (l, 0, j)),
        args=(cc, w_mod, b_mod.reshape(depth, 1, N_MOD * D_MODEL)),
        temporaries=[((D_MODEL, tn), BF16)],
    )


def _cast_kernel(*refs):
    n = len(refs) // 2
    for src, dst in zip(refs[:n], refs[n:]):
        dst[...] = src[...].astype(BF16)


def _cast_weights(w_in, w_conv_out, w_four_out, w_attn_out, w_o):
    depth = w_in.shape[0]
    ins = (w_in, w_conv_out, w_four_out, w_attn_out, w_o)
    cols = (OFF_G,) + tuple(w.shape[2] for w in ins[1:])
    spec = lambda w, c: pl.BlockSpec((None, w.shape[1] // CAST_STEPS, c), lambda l, i: (l, i, 0))
    return _pallas(
        _cast_kernel,
        name="cast_weights",
        out_shape=tuple(jax.ShapeDtypeStruct((depth, w.shape[1], c), BF16) for w, c in zip(ins, cols)),
        grid=(depth, CAST_STEPS),
        in_specs=[spec(w, c) for w, c in zip(ins, cols)],
        out_specs=tuple(spec(w, c) for w, c in zip(ins, cols)),
        args=ins,
    )


def _in_proj_kernel(x_ref, mod_ref, g1_ref, w_ref, gqk_ref, cos_ref, sin_ref, w64_ref,
                    bz_ref, u_ref, fab_ref, qt_ref, k_ref, vt_ref, *fab_scr, sub):
    tm = x_ref.shape[0]
    pitch = _strided_pitch(GRID_W)
    for r0 in range(0, tm, sub):
        rows = slice(r0, r0 + sub)
        h = _modulate(x_ref[rows, :], g1_ref[...], mod_ref[0:1, :], mod_ref[1:2, :]).astype(BF16)
        z = _dot(h, w_ref[...])
        bz_ref[rows, :] = z[:, OFF_B:OFF_C].astype(BF16)
        u_ref[rows, :] = (z[:, OFF_C:OFF_X] * z[:, OFF_X:OFF_F]).astype(BF16)
        fab = _dot(z[:, OFF_F:OFF_Q].astype(BF16), w64_ref[...])
        if fab_scr:
            for a in range(sub // GRID_W):
                start = (r0 // GRID_W + a) * pitch
                _tiled_store(fab_scr[0], slice(start, start + GRID_W), fab[a * GRID_W:(a + 1) * GRID_W, :])
        else:
            fab_ref[rows, :] = fab.astype(BF16)

        lane = lax.broadcasted_iota(jnp.int32, (sub, LANES), 1)
        head_lo = lane < HEAD_DIM
        half_lo = (lane % HEAD_DIM) < ROPE_HALF
        cos = cos_ref[rows, :]
        sin = sin_ref[rows, :]
        for j in range(QK_WIDTH // LANES):
            zj = z[:, OFF_Q + j * LANES:OFF_Q + (j + 1) * LANES]
            sq = zj * zj
            s_lo = jnp.sum(jnp.where(head_lo, sq, 0.0), axis=-1, keepdims=True)
            s_hi = jnp.sum(jnp.where(head_lo, 0.0, sq), axis=-1, keepdims=True)
            r = jnp.where(head_lo, lax.rsqrt(s_lo / HEAD_DIM + EPS), lax.rsqrt(s_hi / HEAD_DIM + EPS))
            n = (zj * r) * gqk_ref[:, j * LANES:(j + 1) * LANES]
            partner = jnp.where(half_lo, pltpu.roll(n, LANES - ROPE_HALF, 1), pltpu.roll(n, ROPE_HALF, 1))
            rot = n * cos + partner * sin
            if j < ATTN_WIDTH // LANES:
                qt_ref[j * LANES:(j + 1) * LANES, rows] = (rot * Q_SCALE).T.astype(BF16)
            else:
                k_ref[rows, :] = rot.astype(BF16)
        vt = z[:, OFF_V:OFF_G].T.astype(BF16)
        for g in range(N_KV_HEADS):
            vt_ref[g * V_ROWS:g * V_ROWS + HEAD_DIM, rows] = vt[g * HEAD_DIM:(g + 1) * HEAD_DIM, :]
            vt_ref[g * V_ROWS + HEAD_DIM:(g + 1) * V_ROWS, rows] = jnp.ones((V_ROWS - HEAD_DIM, sub), BF16)
    if fab_scr:
        for n2 in range(GRID_W):
            fab_ref[n2] = _tiled_load(fab_scr[0], pl.ds(n2, tm // GRID_W, stride=pitch)).astype(BF16)


def _in_proj_call(x2, mod, g1, w1, gqk, cos_t, sin_t, w64, *, layer, batch, seq, tm, shared_mod):
    n = batch * seq
    tps = seq // tm
    mod_idx = (lambda i: (0, 0, 0)) if shared_mod else (lambda i: (i // tps, 0, 0))
    const = lambda i: (0, 0)
    tile_rows = tm // GRID_W
    column_major = tile_rows >= BF16_SUBLANES
    if column_major:
        fab_shape = jax.ShapeDtypeStruct((batch, GRID_W, seq // GRID_W, 2 * FOURIER_WIDTH), BF16)
        fab_spec = pl.BlockSpec((None, GRID_W, tile_rows, 2 * FOURIER_WIDTH), lambda i: (i // tps, 0, i % tps, 0))
        scratch = [pltpu.VMEM((2 * FOURIER_WIDTH // LANES, tile_rows * _strided_pitch(GRID_W), LANES), F32)]
    else:
        fab_shape = jax.ShapeDtypeStruct((n, 2 * FOURIER_WIDTH), BF16)
        fab_spec = pl.BlockSpec((tm, 2 * FOURIER_WIDTH), lambda i: (i, 0))
        scratch = []
    sub = min(tm, IN_PROJ_SUB)
    return _pallas(
        functools.partial(_in_proj_kernel, sub=sub),
        name="in_proj",
        out_shape=(
            jax.ShapeDtypeStruct((n, CONV_WIDTH), BF16),
            jax.ShapeDtypeStruct((n, CONV_WIDTH), BF16),
            fab_shape,
            jax.ShapeDtypeStruct((batch, ATTN_WIDTH, seq), BF16),
            jax.ShapeDtypeStruct((n, KV_WIDTH), BF16),
            jax.ShapeDtypeStruct((batch, N_KV_HEADS * V_ROWS, seq), BF16),
        ),
        grid=(n // tm,),
        in_specs=[
            pl.BlockSpec((tm, D_MODEL), lambda i: (i, 0)),
            pl.BlockSpec((None, N_MOD, D_MODEL), mod_idx),
            pl.BlockSpec((1, D_MODEL), const),
            pl.BlockSpec((None, D_MODEL, OFF_G), lambda i: (layer, 0, 0)),
            pl.BlockSpec((1, QK_WIDTH), const),
            pl.BlockSpec((tm, LANES), lambda i: (i % tps, 0)),
            pl.BlockSpec((tm, LANES), lambda i: (i % tps, 0)),
            pl.BlockSpec((FOURIER_WIDTH, 2 * FOURIER_WIDTH), const),
        ],
        out_specs=(
            pl.BlockSpec((tm, CONV_WIDTH), lambda i: (i, 0)),
            pl.BlockSpec((tm, CONV_WIDTH), lambda i: (i, 0)),
            fab_spec,
            pl.BlockSpec((None, ATTN_WIDTH, tm), lambda i: (i // tps, 0, i % tps)),
            pl.BlockSpec((tm, KV_WIDTH), lambda i: (i, 0)),
            pl.BlockSpec((None, N_KV_HEADS * V_ROWS, tm), lambda i: (i // tps, 0, i % tps)),
        ),
        args=(x2, mod, g1, w1, gqk, cos_t, sin_t, w64),
        scratch=scratch,
        temporaries=[((sub, OFF_G), F32), ((sub, D_MODEL), F32), ((QK_WIDTH + KV_WIDTH, sub), F32)] * 2,
    )


def _fft_kernel(z_ref, cr_ref, sr_ref, twc_ref, tws_ref, cc_ref, sc_ref, y_ref, g_scr, y_scr):
    w = FOURIER_WIDTH
    rows = z_ref.shape[1]
    pitch = _strided_pitch(rows)
    for n2 in range(GRID_W):
        zb = z_ref[n2]
        p = _dot(cr_ref[...], zb)
        q = _dot(sr_ref[...], zb)
        gr = p[:, :w] + q[:, w:]
        gi = p[:, w:] - q[:, :w]
        tc = twc_ref[n2]
        ts = tws_ref[n2]
        g = jnp.concatenate([gr * tc + gi * ts, gi * tc - gr * ts], axis=-1)
        _tiled_store(g_scr, slice(n2 * pitch, n2 * pitch + rows), g)
    for k1 in range(rows):
        slab = _tiled_load(g_scr, pl.ds(k1, GRID_W, stride=pitch)).astype(BF16)
        y = _dot(cc_ref[...], slab[:, :w]) + _dot(sc_ref[...], slab[:, w:])
        _tiled_store(y_scr, pl.ds(k1, GRID_W, stride=pitch), y)
    for k2 in range(GRID_W):
        y_ref[k2 * rows:(k2 + 1) * rows, :] = _tiled_load(y_scr, slice(k2 * pitch, k2 * pitch + rows)).astype(BF16)


def _fft_call(zt, c_r, s_r, twc, tws, c_c, s_c, *, batch, rows):
    seq = rows * GRID_W
    const2 = lambda b: (0, 0)
    const3 = lambda b: (0, 0, 0)
    pitch_rows = GRID_W * _strided_pitch(rows)
    return _pallas(
        _fft_kernel,
        name="fft",
        out_shape=jax.ShapeDtypeStruct((batch * seq, FOURIER_WIDTH), BF16),
        grid=(batch,),
        in_specs=[
            pl.BlockSpec((None, GRID_W, rows, 2 * FOURIER_WIDTH), lambda b: (b, 0, 0, 0)),
            pl.BlockSpec((rows, rows), const2),
            pl.BlockSpec((rows, rows), const2),
            pl.BlockSpec((GRID_W, rows, 1), const3),
            pl.BlockSpec((GRID_W, rows, 1), const3),
            pl.BlockSpec((GRID_W, GRID_W), const2),
            pl.BlockSpec((GRID_W, GRID_W), const2),
        ],
        out_specs=pl.BlockSpec((seq, FOURIER_WIDTH), lambda b: (b, 0)),
        args=(zt, c_r, s_r, twc, tws, c_c, s_c),
        scratch=[pltpu.VMEM((2 * FOURIER_WIDTH // LANES, pitch_rows, LANES), F32),
                 pltpu.VMEM((FOURIER_WIDTH // LANES, pitch_rows, LANES), F32)],
        temporaries=[((rows, 2 * FOURIER_WIDTH), F32)] * 8,
    )


def _dft_tables(n):
    idx = jnp.arange(n, dtype=jnp.int32)
    ang = (2.0 * jnp.pi / n) * ((idx[:, None] * idx[None, :]) % n).astype(F32)
    return jnp.cos(ang), jnp.sin(ang)


def _dft_dense_kernel(z_ref, c_ref, s_ref, y_ref):
    w = FOURIER_WIDTH
    y_ref[...] = (_dot(c_ref[...], z_ref[:, :w]) + _dot(s_ref[...], z_ref[:, w:])).astype(BF16)


def _dft_dense_call(fab, c_l, s_l, *, batch, seq):
    return _pallas(
        _dft_dense_kernel,
        name="dft_dense",
        out_shape=jax.ShapeDtypeStruct((batch * seq, FOURIER_WIDTH), BF16),
        grid=(batch,),
        in_specs=[
            pl.BlockSpec((seq, 2 * FOURIER_WIDTH), lambda b: (b, 0)),
            pl.BlockSpec((seq, seq), lambda b: (0, 0)),
            pl.BlockSpec((seq, seq), lambda b: (0, 0)),
        ],
        out_specs=pl.BlockSpec((seq, FOURIER_WIDTH), lambda b: (b, 0)),
        args=(fab, c_l, s_l),
        temporaries=[((seq, FOURIER_WIDTH), F32)] * 2,
    )


def _fourier_mix(fab, *, batch, seq):
    rows = seq // GRID_W
    scale = (seq * FOURIER_GROUP_DIM) ** -0.5
    if rows < BF16_SUBLANES:
        c_l, s_l = _dft_tables(seq)
        return _dft_dense_call(fab, (c_l * scale).astype(BF16), (s_l * scale).astype(BF16), batch=batch, seq=seq)
    c_r, s_r = _dft_tables(rows)
    c_c, s_c = _dft_tables(GRID_W)
    n2 = jnp.arange(GRID_W, dtype=jnp.int32)
    k1 = jnp.arange(rows, dtype=jnp.int32)
    tw_ang = (2.0 * jnp.pi / seq) * (n2[:, None] * k1[None, :]).astype(F32)[:, :, None]
    return _fft_call(fab, c_r.astype(BF16), s_r.astype(BF16), jnp.cos(tw_ang), jnp.sin(tw_ang),
                     (c_c * scale).astype(BF16), (s_c * scale).astype(BF16), batch=batch, rows=rows)


def _attn_kernel(*refs, chunks, tq, cast_cols):
    n_src = len(chunks)
    n_cast = len(cast_cols)
    qt_ref = refs[0]
    kv_refs = refs[1:1 + 2 * n_src]
    cast_in = refs[1 + 2 * n_src:1 + 2 * n_src + n_cast]
    o_ref = refs[1 + 2 * n_src + n_cast]
    cast_out = refs[2 + 2 * n_src + n_cast:2 + 2 * n_src + 2 * n_cast]
    s_scr = refs[-2:]
    row0 = pl.multiple_of(jnp.minimum(pl.program_id(2), 0), SUBLANES)
    rows = lambda n: pl.ds(row0, n)
    for src, dst, (c0, c1) in zip(cast_in, cast_out, cast_cols):
        dst[...] = src[:, c0:c1].astype(BF16)
    g = pl.program_id(1)
    row = lax.broadcasted_iota(jnp.int32, (KV_WIDTH, tq), 0)
    own_group = (row >= g * HEAD_DIM) & (row < (g + 1) * HEAD_DIM)
    steps = [(src, c) for src, (n_chunks, _) in enumerate(chunks) for c in range(n_chunks)]

    def k_chunk(src, c):
        tk = chunks[src][1]
        return kv_refs[2 * src][c * tk:(c + 1) * tk, :]

    def v_chunk(src, c):
        tk = chunks[src][1]
        return kv_refs[2 * src + 1][:, c * tk:(c + 1) * tk]

    def scores(qz_u, u, kc, slot):
        n = kc.shape[0]
        s = _dot(kc, qz_u)
        s_scr[slot][u, rows(n), :] = s
        part = jnp.max(s.reshape(n // HEAD_DIM, HEAD_DIM, tq), axis=0)
        return jnp.max(part, axis=0, keepdims=True)

    def accumulate(u, carry_u, chunk_max, vc, slot):
        n = vc.shape[1]
        m, acc = carry_u
        m_new = jnp.maximum(m, chunk_max)
        p = jnp.exp2(s_scr[slot][u, rows(n), :] - m_new).astype(BF16)
        return m_new, jnp.exp2(m - m_new) * acc + _dot(vc, p)

    n_tiles = qt_ref.shape[1] // tq
    chains = [(i, h) for i in range(n_tiles) for h in range(GQA_GROUP)]
    qz = [jnp.where(own_group,
                    jnp.concatenate([qt_ref[h * HEAD_DIM:(h + 1) * HEAD_DIM, i * tq:(i + 1) * tq]] * N_KV_HEADS, axis=0),
                    jnp.zeros((), BF16)) for i, h in chains]
    carry = [(jnp.full((1, tq), -jnp.inf, F32), jnp.zeros((V_ROWS, tq), F32)) for _ in chains]
    first_chunk = k_chunk(*steps[0])
    maxima = [scores(qz[u], u, first_chunk, 0) for u in range(len(chains))]
    for t, cur in enumerate(steps):
        slot = t % 2
        vc = v_chunk(*cur)
        kc = k_chunk(*steps[t + 1]) if t + 1 < len(steps) else None
        next_maxima = []
        for u in range(len(chains)):
            if kc is not None:
                next_maxima.append(scores(qz[u], u, kc, 1 - slot))
            carry[u] = accumulate(u, carry[u], maxima[u], vc, slot)
        maxima = next_maxima
    for i in range(n_tiles):
        outs = [acc[:HEAD_DIM, :] / acc[HEAD_DIM:HEAD_DIM + 1, :] for _, acc in carry[i * GQA_GROUP:(i + 1) * GQA_GROUP]]
        o_ref[i * tq:(i + 1) * tq, :] = jnp.concatenate(outs, axis=0).T.astype(BF16)


def _attn_call(qt, sources, *, batch, seq, tq, tk, q_tiles, cast=(), layer=0):
    bq = tq * q_tiles
    nq = seq // bq
    gw = GQA_GROUP * HEAD_DIM
    n_steps = batch * N_KV_HEADS * nq
    step = lambda b, g, i: (b * N_KV_HEADS + g) * nq + i
    in_specs = [pl.BlockSpec((None, gw, bq), lambda b, g, i: (b, g, i))]
    args = [qt]
    chunks = []
    for k, vt in sources:
        ln = k.shape[1]
        ck = min(tk, ln)
        chunks.append((ln // ck, ck))
        in_specs.append(pl.BlockSpec((None, ln, KV_WIDTH), lambda b, g, i: (b, 0, 0)))
        in_specs.append(pl.BlockSpec((None, V_ROWS, ln), lambda b, g, i: (b, g, 0)))
        args += [k, vt]
    out_shape = [jax.ShapeDtypeStruct((batch * seq, ATTN_WIDTH), BF16)]
    out_specs = [pl.BlockSpec((bq, gw), lambda b, g, i: (b * nq + i, g))]
    for w, c0, c1 in cast:
        rows = w.shape[1] // n_steps
        assert rows * n_steps == w.shape[1] and rows % BF16_SUBLANES == 0, w.shape
        in_specs.append(pl.BlockSpec((None, rows, w.shape[2]), lambda b, g, i: (layer, step(b, g, i), 0)))
        args.append(w)
        out_shape.append(jax.ShapeDtypeStruct((w.shape[1], c1 - c0), BF16))
        out_specs.append(pl.BlockSpec((rows, c1 - c0), lambda b, g, i: (step(b, g, i), 0)))
    max_chunk = max(ck for _, ck in chunks)
    return _pallas(
        functools.partial(_attn_kernel, chunks=tuple(chunks), tq=tq, cast_cols=tuple((c0, c1) for _, c0, c1 in cast)),
        name="attention",
        out_shape=tuple(out_shape),
        grid=(batch, N_KV_HEADS, nq),
        in_specs=in_specs,
        out_specs=tuple(out_specs),
        args=args,
        scratch=[pltpu.VMEM((q_tiles * GQA_GROUP, max_chunk, tq), F32)] * 2,
        temporaries=[((max_chunk, tq), F32), ((max_chunk, tq), BF16), ((V_ROWS, tq), F32)] * (q_tiles * GQA_GROUP),
    )


def _merge_mlp_kernel(x_ref, mod_ref, g1_ref, g2_ref, wg_ref, bz_ref, u_ref, up_ref, un_ref, wconv_ref,
                      yf_ref, ya_ref, wc_ref, wf_ref, wa_ref, wo_ref, w1_ref, w2_ref, o_ref, *, tps, sub, ff_chunk):
    tm = x_ref.shape[0]

    pos = pl.program_id(0) % tps
    u = u_ref[...].astype(F32)
    prev_row = jnp.where(pos == 0, 0.0, up_ref[...].astype(F32)[BF16_SUBLANES - 1:BF16_SUBLANES, :])
    next_row = jnp.where(pos == tps - 1, 0.0, un_ref[...].astype(F32)[0:1, :])
    row = lax.broadcasted_iota(jnp.int32, u.shape, 0)
    u_m1 = jnp.where(row == 0, prev_row, pltpu.roll(u, 1, 0))
    u_p1 = jnp.where(row == tm - 1, next_row, pltpu.roll(u, tm - 1, 0))
    conv = wconv_ref[0:1, :] * u_m1 + wconv_ref[1:2, :] * u + wconv_ref[2:3, :] * u_p1
    y_conv = (bz_ref[...].astype(F32) * conv).astype(BF16)

    d = D_MODEL
    for r0 in range(0, tm, sub):
        rows = slice(r0, r0 + sub)
        x = x_ref[rows, :]
        h = _modulate(x, g1_ref[...], mod_ref[0:1, :], mod_ref[1:2, :]).astype(BF16)
        m = _sigmoid(_dot(h, wg_ref[:, 0:d])) * _dot(y_conv[rows, :], wc_ref[...])
        m = m + _sigmoid(_dot(h, wg_ref[:, d:2 * d])) * _dot(yf_ref[rows, :], wf_ref[...])
        m = m + _sigmoid(_dot(h, wg_ref[:, 2 * d:3 * d])) * _dot(ya_ref[rows, :], wa_ref[...])
        x1 = x + mod_ref[2:3, :] * _dot(m.astype(BF16), wo_ref[...])

        h2 = _modulate(x1, g2_ref[...], mod_ref[3:4, :], mod_ref[4:5, :]).astype(BF16)
        acc = jnp.zeros(x1.shape, F32)
        for c in range(D_FF // ff_chunk):
            a = jnp.maximum(_dot(h2, w1_ref[:, c * ff_chunk:(c + 1) * ff_chunk]), 0.0)
            acc = acc + _dot((a * a).astype(BF16), w2_ref[c * ff_chunk:(c + 1) * ff_chunk, :])
        o_ref[rows, :] = x1 + mod_ref[5:6, :] * acc


def _merge_mlp_call(x2, mod, g1, g2, wg, bz, u, wconv, yf, ya, wc, wf, wa, wo, w1, w2, *,
                    layer, batch, seq, tm, shared_mod):
    n = batch * seq
    tps = seq // tm
    hb = tm // BF16_SUBLANES
    n_hb = n // BF16_SUBLANES
    mod_idx = (lambda i: (0, 0, 0)) if shared_mod else (lambda i: (i // tps, 0, 0))
    const = lambda i: (0, 0)
    tile = lambda w: pl.BlockSpec((tm, w), lambda i: (i, 0))
    def weight(w):
        if w.ndim == 2:
            return pl.BlockSpec(w.shape, const, pipeline_mode=pl.Buffered(1))
        return pl.BlockSpec((None,) + w.shape[1:], lambda i: (layer, 0, 0), pipeline_mode=pl.Buffered(1))
    sub = min(tm, MERGE_SUB)
    return _pallas(
        functools.partial(_merge_mlp_kernel, tps=tps, sub=sub, ff_chunk=FF_CHUNK),
        name="merge_mlp",
        out_shape=jax.ShapeDtypeStruct((n, D_MODEL), F32),
        grid=(n // tm,),
        in_specs=[
            tile(D_MODEL),
            pl.BlockSpec((None, N_MOD, D_MODEL), mod_idx),
            pl.BlockSpec((1, D_MODEL), const),
            pl.BlockSpec((1, D_MODEL), const),
            weight(wg),
            tile(CONV_WIDTH),
            tile(CONV_WIDTH),
            pl.BlockSpec((BF16_SUBLANES, CONV_WIDTH), lambda i: (jnp.maximum(i * hb - 1, 0), 0)),
            pl.BlockSpec((BF16_SUBLANES, CONV_WIDTH), lambda i: (jnp.minimum((i + 1) * hb, n_hb - 1), 0)),
            pl.BlockSpec((CONV_K, CONV_WIDTH), const),
            tile(FOURIER_WIDTH),
            tile(ATTN_WIDTH),
            weight(wc),
            weight(wf),
            weight(wa),
            weight(wo),
            weight(w1),
            weight(w2),
        ],
        out_specs=tile(D_MODEL),
        args=(x2, mod, g1, g2, wg, bz, u, u, u, wconv, yf, ya, wc, wf, wa, wo, w1, w2),
        temporaries=[((tm, CONV_WIDTH), F32)] * 4
        + ([((sub, D_MODEL), F32)] * 8 + [((sub, FF_CHUNK), F32), ((sub, FF_CHUNK), BF16)]) * (tm // sub),
    )


def _rope_tables(rows):
    n_freq = ROPE_HALF // 2
    inv = ROPE_THETA ** (-jnp.arange(n_freq, dtype=F32) / n_freq)
    row_ang = jnp.arange(rows, dtype=F32)[:, None] * inv
    col_ang = jnp.arange(GRID_W, dtype=F32)[:, None] * inv
    expand = lambda r, c: jnp.concatenate([jnp.repeat(r, GRID_W, axis=0), jnp.tile(c, (rows, 1))], axis=-1)
    cos = expand(jnp.cos(row_ang), jnp.cos(col_ang))
    sin = expand(jnp.sin(row_ang), jnp.sin(col_ang))
    cos_h = jnp.concatenate([cos, cos], axis=-1)
    sin_h = jnp.concatenate([-sin, sin], axis=-1)
    reps = LANES // HEAD_DIM
    return jnp.tile(cos_h, (1, reps)), jnp.tile(sin_h, (1, reps))


def _channel_dft_matrix():
    c, s = _dft_tables(FOURIER_GROUP_DIM)
    eye = jnp.eye(FOURIER_GROUPS, dtype=F32)
    return jnp.concatenate([jnp.kron(eye, c), jnp.kron(eye, -s)], axis=1).astype(BF16)


def kernel(x, c, ctx, c_ctx, w_mod, b_mod, g_norm1, g_norm2, w_in, w_conv, g_q, g_k,
           w_conv_out, w_four_out, w_attn_out, w_o, w_ff1, w_ff2):
    batch, seq, d = x.shape
    ctx_len = ctx.shape[1]
    depth = w_mod.shape[0]
    rows = seq // GRID_W

    cc = jnp.zeros((MOD_ROWS, d), F32).at[:batch].set(c).at[batch].set(c_ctx)
    mods = _mod_call(cc, w_mod, b_mod)

    cos_t, sin_t = _rope_tables(rows)
    cos_c = jnp.ones((ctx_len, LANES), F32)
    sin_c = jnp.zeros((ctx_len, LANES), F32)
    w64 = _channel_dft_matrix()
    w1, wc, wf, wa, wo = _cast_weights(w_in, w_conv_out, w_four_out, w_attn_out, w_o)
    big_weights = ((w_in, OFF_G, w_in.shape[2]), (w_ff1, 0, w_ff1.shape[2]), (w_ff2, 0, w_ff2.shape[2]))

    xs = x.reshape(batch * seq, d)
    cs = ctx.reshape(batch * ctx_len, d)
    for l in range(depth):
        mod_x = mods[l, :batch].reshape(batch, N_MOD, d)
        mod_c = mods[l, batch:batch + 1].reshape(1, N_MOD, d)
        g1 = g_norm1[l].reshape(1, d)
        g2 = g_norm2[l].reshape(1, d)
        gqk = jnp.concatenate([jnp.tile(g_q[l], N_Q_HEADS), jnp.tile(g_k[l], N_KV_HEADS)]).reshape(1, QK_WIDTH)
        last = l == depth - 1

        bz_c, u_c, fab_c, qt_c, k_c, vt_c = _in_proj_call(
            cs, mod_c, g1, w1, gqk, cos_c, sin_c, w64,
            layer=l, batch=batch, seq=ctx_len, tm=ctx_len, shared_mod=True)
        k_c3 = k_c.reshape(batch, ctx_len, KV_WIDTH)

        bz, u, fab, qt, k, vt = _in_proj_call(
            xs, mod_x, g1, w1, gqk, cos_t, sin_t, w64,
            layer=l, batch=batch, seq=seq, tm=IN_PROJ_TM, shared_mod=False)
        ya, wg, wf1, wf2 = _attn_call(qt, [(k.reshape(batch, seq, KV_WIDTH), vt), (k_c3, vt_c)],
                                      batch=batch, seq=seq, tq=ATTN_TQ, tk=ATTN_TK, q_tiles=ATTN_Q_TILES,
                                      cast=big_weights, layer=l)
        yf = _fourier_mix(fab, batch=batch, seq=seq)
        xs = _merge_mlp_call(xs, mod_x, g1, g2, wg, bz, u, w_conv[l], yf, ya, wc, wf, wa, wo, wf1, wf2,
                             layer=l, batch=batch, seq=seq, tm=MERGE_TM, shared_mod=False)

        if not last:
            (ya_c,) = _attn_call(qt_c, [(k_c3, vt_c)], batch=batch, seq=ctx_len, tq=ATTN_TQ, tk=ATTN_TK,
                                 q_tiles=ctx_len // ATTN_TQ)
            yf_c = _fourier_mix(fab_c, batch=batch, seq=ctx_len)
            cs = _merge_mlp_call(cs, mod_c, g1, g2, wg, bz_c, u_c, w_conv[l], yf_c, ya_c, wc, wf, wa, wo, wf1, wf2,
                                 layer=l, batch=batch, seq=ctx_len, tm=ctx_len, shared_mod=True)
    return xs.reshape(batch, seq, d)
```

```python
import functools
import math

import jax
import jax.numpy as jnp
from jax import lax
from jax.experimental import pallas as pl
from jax.experimental.pallas import tpu as pltpu

D_MODEL = 1024
GRID_W = 64
HEAD_DIM = 64
N_Q_HEADS = 8
N_KV_HEADS = 2
GQA_GROUP = N_Q_HEADS // N_KV_HEADS
ATTN_WIDTH = N_Q_HEADS * HEAD_DIM
KV_WIDTH = N_KV_HEADS * HEAD_DIM
CONV_WIDTH = 256
CONV_K = 3
FOURIER_GROUPS = 4
FOURIER_GROUP_DIM = 64
FOURIER_WIDTH = FOURIER_GROUPS * FOURIER_GROUP_DIM
N_BRANCHES = 3
D_FF = 4 * D_MODEL
ROPE_THETA = 10000.0
ROPE_HALF = HEAD_DIM // 2
EPS = 1e-6
N_MOD = 6

OFF_B = 0
OFF_C = OFF_B + CONV_WIDTH
OFF_X = OFF_C + CONV_WIDTH
OFF_F = OFF_X + CONV_WIDTH
OFF_Q = OFF_F + FOURIER_WIDTH
OFF_K = OFF_Q + ATTN_WIDTH
OFF_V = OFF_K + KV_WIDTH
OFF_G = OFF_V + KV_WIDTH

LANES = 128
SUBLANES = 8
BF16_SUBLANES = 16
QK_WIDTH = ATTN_WIDTH + KV_WIDTH
V_ROWS = HEAD_DIM + BF16_SUBLANES
Q_SCALE = HEAD_DIM ** -0.5 * 1.4426950408889634
ATTN_TQ = 256
ATTN_TK = 512
ATTN_Q_TILES = 2
MOD_TN = 1024
FF_CHUNK = 1024
CAST_STEPS = 8
IN_PROJ_TM = 1024
MERGE_TM = 512
IN_PROJ_SUB = 256
MERGE_LAG = 2
MERGE_SUB = 256
MOD_ROWS = 8

BF16 = jnp.bfloat16
F32 = jnp.float32


def _dot(a, b):
    return jnp.dot(a, b, preferred_element_type=F32)


def _tile_bytes(shape, dtype):
    itemsize = jnp.dtype(dtype).itemsize
    dims = [1 if d is None else d for d in shape]
    sublanes = SUBLANES * (4 // itemsize)
    dims[-1] = -(-dims[-1] // LANES) * LANES
    if len(dims) > 1:
        dims[-2] = -(-dims[-2] // sublanes) * sublanes
    return math.prod(dims) * itemsize


def _pallas(kernel, *, name, grid, in_specs, out_specs, out_shape, args, scratch=(), temporaries=()):
    outs = out_shape if isinstance(out_shape, tuple) else (out_shape,)
    ospecs = out_specs if isinstance(out_specs, tuple) else (out_specs,)
    total = 0
    for spec, dtype in [(s, a.dtype) for s, a in zip(in_specs, args)] + [(s, o.dtype) for s, o in zip(ospecs, outs)]:
        buffers = spec.pipeline_mode.buffer_count if spec.pipeline_mode is not None else 2
        total += buffers * _tile_bytes(spec.block_shape, dtype)
    total += sum(_tile_bytes(s.shape, s.dtype) for s in scratch)
    total += sum(_tile_bytes(shape, dtype) for shape, dtype in temporaries)
    return pl.pallas_call(
        kernel, out_shape=out_shape, grid=grid, in_specs=in_specs, out_specs=out_specs, scratch_shapes=list(scratch),
        compiler_params=pltpu.CompilerParams(dimension_semantics=("arbitrary",) * len(grid), vmem_limit_bytes=total),
        name=name,
    )(*args)


def _modulate(x, g, shift, scale):
    ms = jnp.mean(x * x, axis=-1, keepdims=True)
    y = x * lax.rsqrt(ms + EPS)
    return (y * g) * (1.0 + scale) + shift


def _sigmoid(x):
    return 1.0 / (1.0 + jnp.exp(-x))


def _strided_pitch(n):
    tiles = -(-n // SUBLANES)
    return SUBLANES * (tiles if tiles % 2 else tiles + 1)


def _tiled_store(scr, rows, value):
    for j in range(scr.shape[0]):
        scr[j, rows, :] = value[:, j * LANES:(j + 1) * LANES]


def _tiled_load(scr, rows):
    return jnp.concatenate([scr[j, rows, :] for j in range(scr.shape[0])], axis=-1)


def _mod_kernel(c_ref, w_ref, b_ref, o_ref):
    c = c_ref[...]
    s = (c * _sigmoid(c)).astype(BF16)
    o_ref[...] = _dot(s, w_ref[...].astype(BF16)) + b_ref[...]


def _mod_call(cc, w_mod, b_mod):
    depth = w_mod.shape[0]
    tn = MOD_TN
    return _pallas(
        _mod_kernel,
        name="mod",
        out_shape=jax.ShapeDtypeStruct((depth, MOD_ROWS, N_MOD * D_MODEL), F32),
        grid=(depth, N_MOD * D_MODEL // tn),
        in_specs=[
            pl.BlockSpec((MOD_ROWS, D_MODEL), lambda l, j: (0, 0)),
            pl.BlockSpec((None, D_MODEL, tn), lambda l, j: (l, 0, j)),
            pl.BlockSpec((None, 1, tn), lambda l, j: (l, 0, j)),
        ],
        out_specs=pl.BlockSpec((None, MOD_ROWS, tn), lambda l, j: (l, 0, j)),
        args=(cc, w_mod, b_mod.reshape(depth, 1, N_MOD * D_MODEL)),
        temporaries=[((D_MODEL, tn), BF16)],
    )


def _cast_kernel(*refs):
    n = len(refs) // 2
    for src, dst in zip(refs[:n], refs[n:]):
        dst[...] = src[...].astype(BF16)


def _cast_weights(w_in, w_conv_out, w_four_out, w_attn_out, w_o):
    depth = w_in.shape[0]
    ins = (w_in, w_conv_out, w_four_out, w_attn_out, w_o)
    cols = (OFF_G,) + tuple(w.shape[2] for w in ins[1:])
    spec = lambda w, c: pl.BlockSpec((None, w.shape[1] // CAST_STEPS, c), lambda l, i: (l, i, 0))
    return _pallas(
        _cast_kernel,
        name="cast_weights",
        out_shape=tuple(jax.ShapeDtypeStruct((depth, w.shape[1], c), BF16) for w, c in zip(ins, cols)),
        grid=(depth, CAST_STEPS),
        in_specs=[spec(w, c) for w, c in zip(ins, cols)],
        out_specs=tuple(spec(w, c) for w, c in zip(ins, cols)),
        args=ins,
    )


def _in_proj_kernel(x_ref, mod_ref, g1_ref, w_ref, gqk_ref, cos_ref, sin_ref, w64_ref,
                    bz_ref, u_ref, fab_ref, qt_ref, k_ref, vt_ref, *fab_scr, sub):
    tm = x_ref.shape[0]
    pitch = _strided_pitch(GRID_W)
    for r0 in range(0, tm, sub):
        rows = slice(r0, r0 + sub)
        h = _modulate(x_ref[rows, :], g1_ref[...], mod_ref[0:1, :], mod_ref[1:2, :]).astype(BF16)
        z = _dot(h, w_ref[...])
        bz_ref[rows, :] = z[:, OFF_B:OFF_C].astype(BF16)
        u_ref[rows, :] = (z[:, OFF_C:OFF_X] * z[:, OFF_X:OFF_F]).astype(BF16)
        fab = _dot(z[:, OFF_F:OFF_Q].astype(BF16), w64_ref[...])
        if fab_scr:
            for a in range(sub // GRID_W):
                start = (r0 // GRID_W + a) * pitch
                _tiled_store(fab_scr[0], slice(start, start + GRID_W), fab[a * GRID_W:(a + 1) * GRID_W, :])
        else:
            fab_ref[rows, :] = fab.astype(BF16)

        lane = lax.broadcasted_iota(jnp.int32, (sub, LANES), 1)
        head_lo = lane < HEAD_DIM
        half_lo = (lane % HEAD_DIM) < ROPE_HALF
        cos = cos_ref[rows, :]
        sin = sin_ref[rows, :]
        for j in range(QK_WIDTH // LANES):
            zj = z[:, OFF_Q + j * LANES:OFF_Q + (j + 1) * LANES]
            sq = zj * zj
            s_lo = jnp.sum(jnp.where(head_lo, sq, 0.0), axis=-1, keepdims=True)
            s_hi = jnp.sum(jnp.where(head_lo, 0.0, sq), axis=-1, keepdims=True)
            r = jnp.where(head_lo, lax.rsqrt(s_lo / HEAD_DIM + EPS), lax.rsqrt(s_hi / HEAD_DIM + EPS))
            n = (zj * r) * gqk_ref[:, j * LANES:(j + 1) * LANES]
            partner = jnp.where(half_lo, pltpu.roll(n, LANES - ROPE_HALF, 1), pltpu.roll(n, ROPE_HALF, 1))
            rot = n * cos + partner * sin
            if j < ATTN_WIDTH // LANES:
                qt_ref[j * LANES:(j + 1) * LANES, rows] = (rot * Q_SCALE).T.astype(BF16)
            else:
                k_ref[rows, :] = rot.astype(BF16)
        vt = z[:, OFF_V:OFF_G].T.astype(BF16)
        for g in range(N_KV_HEADS):
            vt_ref[g * V_ROWS:g * V_ROWS + HEAD_DIM, rows] = vt[g * HEAD_DIM:(g + 1) * HEAD_DIM, :]
            vt_ref[g * V_ROWS + HEAD_DIM:(g + 1) * V_ROWS, rows] = jnp.ones((V_ROWS - HEAD_DIM, sub), BF16)
    if fab_scr:
        for n2 in range(GRID_W):
            fab_ref[n2] = _tiled_load(fab_scr[0], pl.ds(n2, tm // GRID_W, stride=pitch)).astype(BF16)


def _in_proj_call(x2, mod, g1, w1, gqk, cos_t, sin_t, w64, *, layer, batch, seq, tm, shared_mod):
    n = batch * seq
    tps = seq // tm
    mod_idx = (lambda i: (0, 0, 0)) if shared_mod else (lambda i: (i // tps, 0, 0))
    const = lambda i: (0, 0)
    tile_rows = tm // GRID_W
    column_major = tile_rows >= BF16_SUBLANES
    if column_major:
        fab_shape = jax.ShapeDtypeStruct((batch, GRID_W, seq // GRID_W, 2 * FOURIER_WIDTH), BF16)
        fab_spec = pl.BlockSpec((None, GRID_W, tile_rows, 2 * FOURIER_WIDTH), lambda i: (i // tps, 0, i % tps, 0))
        scratch = [pltpu.VMEM((2 * FOURIER_WIDTH // LANES, tile_rows * _strided_pitch(GRID_W), LANES), F32)]
    else:
        fab_shape = jax.ShapeDtypeStruct((n, 2 * FOURIER_WIDTH), BF16)
        fab_spec = pl.BlockSpec((tm, 2 * FOURIER_WIDTH), lambda i: (i, 0))
        scratch = []
    sub = min(tm, IN_PROJ_SUB)
    return _pallas(
        functools.partial(_in_proj_kernel, sub=sub),
        name="in_proj",
        out_shape=(
            jax.ShapeDtypeStruct((n, CONV_WIDTH), BF16),
            jax.ShapeDtypeStruct((n, CONV_WIDTH), BF16),
            fab_shape,
            jax.ShapeDtypeStruct((batch, ATTN_WIDTH, seq), BF16),
            jax.ShapeDtypeStruct((n, KV_WIDTH), BF16),
            jax.ShapeDtypeStruct((batch, N_KV_HEADS * V_ROWS, seq), BF16),
        ),
        grid=(n // tm,),
        in_specs=[
            pl.BlockSpec((tm, D_MODEL), lambda i: (i, 0)),
            pl.BlockSpec((None, N_MOD, D_MODEL), mod_idx),
            pl.BlockSpec((1, D_MODEL), const),
            pl.BlockSpec((None, D_MODEL, OFF_G), lambda i: (layer, 0, 0)),
            pl.BlockSpec((1, QK_WIDTH), const),
            pl.BlockSpec((tm, LANES), lambda i: (i % tps, 0)),
            pl.BlockSpec((tm, LANES), lambda i: (i % tps, 0)),
            pl.BlockSpec((FOURIER_WIDTH, 2 * FOURIER_WIDTH), const),
        ],
        out_specs=(
            pl.BlockSpec((tm, CONV_WIDTH), lambda i: (i, 0)),
            pl.BlockSpec((tm, CONV_WIDTH), lambda i: (i, 0)),
            fab_spec,
            pl.BlockSpec((None, ATTN_WIDTH, tm), lambda i: (i // tps, 0, i % tps)),
            pl.BlockSpec((tm, KV_WIDTH), lambda i: (i, 0)),
            pl.BlockSpec((None, N_KV_HEADS * V_ROWS, tm), lambda i: (i // tps, 0, i % tps)),
        ),
        args=(x2, mod, g1, w1, gqk, cos_t, sin_t, w64),
        scratch=scratch,
        temporaries=[((sub, OFF_G), F32), ((sub, D_MODEL), F32), ((QK_WIDTH + KV_WIDTH, sub), F32)] * 2,
    )


def _fft_kernel(z_ref, cr_ref, sr_ref, twc_ref, tws_ref, cc_ref, sc_ref, y_ref, g_scr, y_scr):
    w = FOURIER_WIDTH
    rows = z_ref.shape[1]
    pitch = _strided_pitch(rows)
    for n2 in range(GRID_W):
        zb = z_ref[n2]
        p = _dot(cr_ref[...], zb)
        q = _dot(sr_ref[...], zb)
        gr = p[:, :w] + q[:, w:]
        gi = p[:, w:] - q[:, :w]
        tc = twc_ref[n2]
        ts = tws_ref[n2]
        g = jnp.concatenate([gr * tc + gi * ts, gi * tc - gr * ts], axis=-1)
        _tiled_store(g_scr, slice(n2 * pitch, n2 * pitch + rows), g)
    for k1 in range(rows):
        slab = _tiled_load(g_scr, pl.ds(k1, GRID_W, stride=pitch)).astype(BF16)
        y = _dot(cc_ref[...], slab[:, :w]) + _dot(sc_ref[...], slab[:, w:])
        _tiled_store(y_scr, pl.ds(k1, GRID_W, stride=pitch), y)
    for k2 in range(GRID_W):
        y_ref[k2 * rows:(k2 + 1) * rows, :] = _tiled_load(y_scr, slice(k2 * pitch, k2 * pitch + rows)).astype(BF16)


def _fft_call(zt, c_r, s_r, twc, tws, c_c, s_c, *, batch, rows):
    seq = rows * GRID_W
    const2 = lambda b: (0, 0)
    const3 = lambda b: (0, 0, 0)
    pitch_rows = GRID_W * _strided_pitch(rows)
    return _pallas(
        _fft_kernel,
        name="fft",
        out_shape=jax.ShapeDtypeStruct((batch * seq, FOURIER_WIDTH), BF16),
        grid=(batch,),
        in_specs=[
            pl.BlockSpec((None, GRID_W, rows, 2 * FOURIER_WIDTH), lambda b: (b, 0, 0, 0)),
            pl.BlockSpec((rows, rows), const2),
            pl.BlockSpec((rows, rows), const2),
            pl.BlockSpec((GRID_W, rows, 1), const3),
            pl.BlockSpec((GRID_W, rows, 1), const3),
            pl.BlockSpec((GRID_W, GRID_W), const2),
            pl.BlockSpec((GRID_W, GRID_W), const2),
        ],
        out_specs=pl.BlockSpec((seq, FOURIER_WIDTH), lambda b: (b, 0)),
        args=(zt, c_r, s_r, twc, tws, c_c, s_c),
        scratch=[pltpu.VMEM((2 * FOURIER_WIDTH // LANES, pitch_rows, LANES), F32),
                 pltpu.VMEM((FOURIER_WIDTH // LANES, pitch_rows, LANES), F32)],
        temporaries=[((rows, 2 * FOURIER_WIDTH), F32)] * 8,
    )


def _dft_tables(n):
    idx = jnp.arange(n, dtype=jnp.int32)
    ang = (2.0 * jnp.pi / n) * ((idx[:, None] * idx[None, :]) % n).astype(F32)
    return jnp.cos(ang), jnp.sin(ang)


def _dft_dense_kernel(z_ref, c_ref, s_ref, y_ref):
    w = FOURIER_WIDTH
    y_ref[...] = (_dot(c_ref[...], z_ref[:, :w]) + _dot(s_ref[...], z_ref[:, w:])).astype(BF16)


def _dft_dense_call(fab, c_l, s_l, *, batch, seq):
    return _pallas(
        _dft_dense_kernel,
        name="dft_dense",
        out_shape=jax.ShapeDtypeStruct((batch * seq, FOURIER_WIDTH), BF16),
        grid=(batch,),
        in_specs=[
            pl.BlockSpec((seq, 2 * FOURIER_WIDTH), lambda b: (b, 0)),
            pl.BlockSpec((seq, seq), lambda b: (0, 0)),
            pl.BlockSpec((seq, seq), lambda b: (0, 0)),
        ],
        out_specs=pl.BlockSpec((seq, FOURIER_WIDTH), lambda b: (b, 0)),
        args=(fab, c_l, s_l),
        temporaries=[((seq, FOURIER_WIDTH), F32)] * 2,
    )


def _fourier_mix(fab, *, batch, seq):
    rows = seq // GRID_W
    scale = (seq * FOURIER_GROUP_DIM) ** -0.5
    if rows < BF16_SUBLANES:
        c_l, s_l = _dft_tables(seq)
        return _dft_dense_call(fab, (c_l * scale).astype(BF16), (s_l * scale).astype(BF16), batch=batch, seq=seq)
    c_r, s_r = _dft_tables(rows)
    c_c, s_c = _dft_tables(GRID_W)
    n2 = jnp.arange(GRID_W, dtype=jnp.int32)
    k1 = jnp.arange(rows, dtype=jnp.int32)
    tw_ang = (2.0 * jnp.pi / seq) * (n2[:, None] * k1[None, :]).astype(F32)[:, :, None]
    return _fft_call(fab, c_r.astype(BF16), s_r.astype(BF16), jnp.cos(tw_ang), jnp.sin(tw_ang),
                     (c_c * scale).astype(BF16), (s_c * scale).astype(BF16), batch=batch, rows=rows)


def _attn_kernel(*refs, chunks, tq, cast_cols):
    n_src = len(chunks)
    n_cast = len(cast_cols)
    qt_ref = refs[0]
    kv_refs = refs[1:1 + 2 * n_src]
    cast_in = refs[1 + 2 * n_src:1 + 2 * n_src + n_cast]
    o_ref = refs[1 + 2 * n_src + n_cast]
    cast_out = refs[2 + 2 * n_src + n_cast:2 + 2 * n_src + 2 * n_cast]
    s_scr = refs[-2:]
    row0 = pl.multiple_of(jnp.minimum(pl.program_id(2), 0), SUBLANES)
    rows = lambda n: pl.ds(row0, n)
    for src, dst, (c0, c1) in zip(cast_in, cast_out, cast_cols):
        dst[...] = src[:, c0:c1].astype(BF16)
    g = pl.program_id(1)
    row = lax.broadcasted_iota(jnp.int32, (KV_WIDTH, tq), 0)
    own_group = (row >= g * HEAD_DIM) & (row < (g + 1) * HEAD_DIM)
    steps = [(src, c) for src, (n_chunks, _) in enumerate(chunks) for c in range(n_chunks)]

    def k_chunk(src, c):
        tk = chunks[src][1]
        return kv_refs[2 * src][c * tk:(c + 1) * tk, :]

    def v_chunk(src, c):
        tk = chunks[src][1]
        return kv_refs[2 * src + 1][:, c * tk:(c + 1) * tk]

    def scores(qz_u, u, kc, slot):
        n = kc.shape[0]
        s = _dot(kc, qz_u)
        s_scr[slot][u, rows(n), :] = s
        part = jnp.max(s.reshape(n // HEAD_DIM, HEAD_DIM, tq), axis=0)
        return jnp.max(part, axis=0, keepdims=True)

    def accumulate(u, carry_u, chunk_max, vc, slot):
        n = vc.shape[1]
        m, acc = carry_u
        m_new = jnp.maximum(m, chunk_max)
        p = jnp.exp2(s_scr[slot][u, rows(n), :] - m_new).astype(BF16)
        return m_new, jnp.exp2(m - m_new) * acc + _dot(vc, p)

    n_tiles = qt_ref.shape[1] // tq
    chains = [(i, h) for i in range(n_tiles) for h in range(GQA_GROUP)]
    qz = [jnp.where(own_group,
                    jnp.concatenate([qt_ref[h * HEAD_DIM:(h + 1) * HEAD_DIM, i * tq:(i + 1) * tq]] * N_KV_HEADS, axis=0),
                    jnp.zeros((), BF16)) for i, h in chains]
    carry = [(jnp.full((1, tq), -jnp.inf, F32), jnp.zeros((V_ROWS, tq), F32)) for _ in chains]
    first_chunk = k_chunk(*steps[0])
    maxima = [scores(qz[u], u, first_chunk, 0) for u in range(len(chains))]
    for t, cur in enumerate(steps):
        slot = t % 2
        vc = v_chunk(*cur)
        kc = k_chunk(*steps[t + 1]) if t + 1 < len(steps) else None
        next_maxima = []
        for u in range(len(chains)):
            if kc is not None:
                next_maxima.append(scores(qz[u], u, kc, 1 - slot))
            carry[u] = accumulate(u, carry[u], maxima[u], vc, slot)
        maxima = next_maxima
    for i in range(n_tiles):
        outs = [acc[:HEAD_DIM, :] / acc[HEAD_DIM:HEAD_DIM + 1, :] for _, acc in carry[i * GQA_GROUP:(i + 1) * GQA_GROUP]]
        o_ref[i * tq:(i + 1) * tq, :] = jnp.concatenate(outs, axis=0).T.astype(BF16)


def _attn_call(qt, sources, *, batch, seq, tq, tk, q_tiles, cast=(), layer=0):
    bq = tq * q_tiles
    nq = seq // bq
    gw = GQA_GROUP * HEAD_DIM
    n_steps = batch * N_KV_HEADS * nq
    step = lambda b, g, i: (b * N_KV_HEADS + g) * nq + i
    in_specs = [pl.BlockSpec((None, gw, bq), lambda b, g, i: (b, g, i))]
    args = [qt]
    chunks = []
    for k, vt in sources:
        ln = k.shape[1]
        ck = min(tk, ln)
        chunks.append((ln // ck, ck))
        in_specs.append(pl.BlockSpec((None, ln, KV_WIDTH), lambda b, g, i: (b, 0, 0)))
        in_specs.append(pl.BlockSpec((None, V_ROWS, ln), lambda b, g, i: (b, g, 0)))
        args += [k, vt]
    out_shape = [jax.ShapeDtypeStruct((batch * seq, ATTN_WIDTH), BF16)]
    out_specs = [pl.BlockSpec((bq, gw), lambda b, g, i: (b * nq + i, g))]
    for w, c0, c1 in cast:
        rows = w.shape[1] // n_steps
        assert rows * n_steps == w.shape[1] and rows % BF16_SUBLANES == 0, w.shape
        in_specs.append(pl.BlockSpec((None, rows, w.shape[2]), lambda b, g, i: (layer, step(b, g, i), 0)))
        args.append(w)
        out_shape.append(jax.ShapeDtypeStruct((w.shape[1], c1 - c0), BF16))
        out_specs.append(pl.BlockSpec((rows, c1 - c0), lambda b, g, i: (step(b, g, i), 0)))
    max_chunk = max(ck for _, ck in chunks)
    return _pallas(
        functools.partial(_attn_kernel, chunks=tuple(chunks), tq=tq, cast_cols=tuple((c0, c1) for _, c0, c1 in cast)),
        name="attention",
        out_shape=tuple(out_shape),
        grid=(batch, N_KV_HEADS, nq),
        in_specs=in_specs,
        out_specs=tuple(out_specs),
        args=args,
        scratch=[pltpu.VMEM((q_tiles * GQA_GROUP, max_chunk, tq), F32)] * 2,
        temporaries=[((max_chunk, tq), F32), ((max_chunk, tq), BF16), ((V_ROWS, tq), F32)] * (q_tiles * GQA_GROUP),
    )


def _merge_mlp_kernel(x_ref, mod_ref, g1_ref, g2_ref, wg_ref, bz_ref, u_ref, up_ref, un_ref, wconv_ref,
                      yf_ref, ya_ref, wc_ref, wf_ref, wa_ref, wo_ref, w1_ref, w2_ref, o_ref, *, tps, sub, ff_chunk):
    tm = x_ref.shape[0]

    pos = pl.program_id(0) % tps
    u = u_ref[...].astype(F32)
    prev_row = jnp.where(pos == 0, 0.0, up_ref[...].astype(F32)[BF16_SUBLANES - 1:BF16_SUBLANES, :])
    next_row = jnp.where(pos == tps - 1, 0.0, un_ref[...].astype(F32)[0:1, :])
    row = lax.broadcasted_iota(jnp.int32, u.shape, 0)
    u_m1 = jnp.where(row == 0, prev_row, pltpu.roll(u, 1, 0))
    u_p1 = jnp.where(row == tm - 1, next_row, pltpu.roll(u, tm - 1, 0))
    conv = wconv_ref[0:1, :] * u_m1 + wconv_ref[1:2, :] * u + wconv_ref[2:3, :] * u_p1
    y_conv = (bz_ref[...].astype(F32) * conv).astype(BF16)

    d = D_MODEL

    def chain(r0):
        rows = slice(r0, r0 + sub)
        x = x_ref[rows, :]
        h = _modulate(x, g1_ref[...], mod_ref[0:1, :], mod_ref[1:2, :]).astype(BF16)
        yield
        m = _sigmoid(_dot(h, wg_ref[:, 0:d])) * _dot(y_conv[rows, :], wc_ref[...])
        yield
        m = m + _sigmoid(_dot(h, wg_ref[:, d:2 * d])) * _dot(yf_ref[rows, :], wf_ref[...])
        yield
        m = m + _sigmoid(_dot(h, wg_ref[:, 2 * d:3 * d])) * _dot(ya_ref[rows, :], wa_ref[...])
        yield
        x1 = x + mod_ref[2:3, :] * _dot(m.astype(BF16), wo_ref[...])
        h2 = _modulate(x1, g2_ref[...], mod_ref[3:4, :], mod_ref[4:5, :]).astype(BF16)
        acc = jnp.zeros(x1.shape, F32)
        for c in range(D_FF // ff_chunk):
            yield
            a = jnp.maximum(_dot(h2, w1_ref[:, c * ff_chunk:(c + 1) * ff_chunk]), 0.0)
            acc = acc + _dot((a * a).astype(BF16), w2_ref[c * ff_chunk:(c + 1) * ff_chunk, :])
        o_ref[rows, :] = x1 + mod_ref[5:6, :] * acc

    chains = [chain(r0) for r0 in range(0, tm, sub)]
    live = set(range(len(chains)))
    tick = 0
    while live:
        for i in sorted(live):
            if tick >= i * MERGE_LAG and next(chains[i], "done") == "done":
                live.discard(i)
        tick += 1


def _merge_mlp_call(x2, mod, g1, g2, wg, bz, u, wconv, yf, ya, wc, wf, wa, wo, w1, w2, *,
                    layer, batch, seq, tm, shared_mod):
    n = batch * seq
    tps = seq // tm
    hb = tm // BF16_SUBLANES
    n_hb = n // BF16_SUBLANES
    mod_idx = (lambda i: (0, 0, 0)) if shared_mod else (lambda i: (i // tps, 0, 0))
    const = lambda i: (0, 0)
    tile = lambda w: pl.BlockSpec((tm, w), lambda i: (i, 0))
    def weight(w):
        if w.ndim == 2:
            return pl.BlockSpec(w.shape, const, pipeline_mode=pl.Buffered(1))
        return pl.BlockSpec((None,) + w.shape[1:], lambda i: (layer, 0, 0), pipeline_mode=pl.Buffered(1))
    sub = min(tm, MERGE_SUB)
    return _pallas(
        functools.partial(_merge_mlp_kernel, tps=tps, sub=sub, ff_chunk=FF_CHUNK),
        name="merge_mlp",
        out_shape=jax.ShapeDtypeStruct((n, D_MODEL), F32),
        grid=(n // tm,),
        in_specs=[
            tile(D_MODEL),
            pl.BlockSpec((None, N_MOD, D_MODEL), mod_idx),
            pl.BlockSpec((1, D_MODEL), const),
            pl.BlockSpec((1, D_MODEL), const),
            weight(wg),
            tile(CONV_WIDTH),
            tile(CONV_WIDTH),
            pl.BlockSpec((BF16_SUBLANES, CONV_WIDTH), lambda i: (jnp.maximum(i * hb - 1, 0), 0)),
            pl.BlockSpec((BF16_SUBLANES, CONV_WIDTH), lambda i: (jnp.minimum((i + 1) * hb, n_hb - 1), 0)),
            pl.BlockSpec((CONV_K, CONV_WIDTH), const),
            tile(FOURIER_WIDTH),
            tile(ATTN_WIDTH),
            weight(wc),
            weight(wf),
            weight(wa),
            weight(wo),
            weight(w1),
            weight(w2),
        ],
        out_specs=tile(D_MODEL),
        args=(x2, mod, g1, g2, wg, bz, u, u, u, wconv, yf, ya, wc, wf, wa, wo, w1, w2),
        temporaries=[((tm, CONV_WIDTH), F32)] * 4
        + ([((sub, D_MODEL), F32)] * 8 + [((sub, FF_CHUNK), F32), ((sub, FF_CHUNK), BF16)]) * (tm // sub),
    )


def _rope_tables(rows):
    n_freq = ROPE_HALF // 2
    inv = ROPE_THETA ** (-jnp.arange(n_freq, dtype=F32) / n_freq)
    row_ang = jnp.arange(rows, dtype=F32)[:, None] * inv
    col_ang = jnp.arange(GRID_W, dtype=F32)[:, None] * inv
    expand = lambda r, c: jnp.concatenate([jnp.repeat(r, GRID_W, axis=0), jnp.tile(c, (rows, 1))], axis=-1)
    cos = expand(jnp.cos(row_ang), jnp.cos(col_ang))
    sin = expand(jnp.sin(row_ang), jnp.sin(col_ang))
    cos_h = jnp.concatenate([cos, cos], axis=-1)
    sin_h = jnp.concatenate([-sin, sin], axis=-1)
    reps = LANES // HEAD_DIM
    return jnp.tile(cos_h, (1, reps)), jnp.tile(sin_h, (1, reps))


def _channel_dft_matrix():
    c, s = _dft_tables(FOURIER_GROUP_DIM)
    eye = jnp.eye(FOURIER_GROUPS, dtype=F32)
    return jnp.concatenate([jnp.kron(eye, c), jnp.kron(eye, -s)], axis=1).astype(BF16)


def kernel(x, c, ctx, c_ctx, w_mod, b_mod, g_norm1, g_norm2, w_in, w_conv, g_q, g_k,
           w_conv_out, w_four_out, w_attn_out, w_o, w_ff1, w_ff2):
    batch, seq, d = x.shape
    ctx_len = ctx.shape[1]
    depth = w_mod.shape[0]
    rows = seq // GRID_W

    cc = jnp.zeros((MOD_ROWS, d), F32).at[:batch].set(c).at[batch].set(c_ctx)
    mods = _mod_call(cc, w_mod, b_mod)

    cos_t, sin_t = _rope_tables(rows)
    cos_c = jnp.ones((ctx_len, LANES), F32)
    sin_c = jnp.zeros((ctx_len, LANES), F32)
    w64 = _channel_dft_matrix()
    w1, wc, wf, wa, wo = _cast_weights(w_in, w_conv_out, w_four_out, w_attn_out, w_o)
    big_weights = ((w_in, OFF_G, w_in.shape[2]), (w_ff1, 0, w_ff1.shape[2]), (w_ff2, 0, w_ff2.shape[2]))

    xs = x.reshape(batch * seq, d)
    cs = ctx.reshape(batch * ctx_len, d)
    for l in range(depth):
        mod_x = mods[l, :batch].reshape(batch, N_MOD, d)
        mod_c = mods[l, batch:batch + 1].reshape(1, N_MOD, d)
        g1 = g_norm1[l].reshape(1, d)
        g2 = g_norm2[l].reshape(1, d)
        gqk = jnp.concatenate([jnp.tile(g_q[l], N_Q_HEADS), jnp.tile(g_k[l], N_KV_HEADS)]).reshape(1, QK_WIDTH)
        last = l == depth - 1

        bz_c, u_c, fab_c, qt_c, k_c, vt_c = _in_proj_call(
            cs, mod_c, g1, w1, gqk, cos_c, sin_c, w64,
            layer=l, batch=batch, seq=ctx_len, tm=ctx_len, shared_mod=True)
        k_c3 = k_c.reshape(batch, ctx_len, KV_WIDTH)

        bz, u, fab, qt, k, vt = _in_proj_call(
            xs, mod_x, g1, w1, gqk, cos_t, sin_t, w64,
            layer=l, batch=batch, seq=seq, tm=IN_PROJ_TM, shared_mod=False)
        ya, wg, wf1, wf2 = _attn_call(qt, [(k.reshape(batch, seq, KV_WIDTH), vt), (k_c3, vt_c)],
                                      batch=batch, seq=seq, tq=ATTN_TQ, tk=ATTN_TK, q_tiles=ATTN_Q_TILES,
                                      cast=big_weights, layer=l)
        yf = _fourier_mix(fab, batch=batch, seq=seq)
        xs = _merge_mlp_call(xs, mod_x, g1, g2, wg, bz, u, w_conv[l], yf, ya, wc, wf, wa, wo, wf1, wf2,
                             layer=l, batch=batch, seq=seq, tm=MERGE_TM, shared_mod=False)

        if not last:
            (ya_c,) = _attn_call(qt_c, [(k_c3, vt_c)], batch=batch, seq=ctx_len, tq=ATTN_TQ, tk=ATTN_TK,
                                 q_tiles=ctx_len // ATTN_TQ)
            yf_c = _fourier_mix(fab_c, batch=batch, seq=ctx_len)
            cs = _merge_mlp_call(cs, mod_c, g1, g2, wg, bz_c, u_c, w_conv[l], yf_c, ya_c, wc, wf, wa, wo, wf1, wf2,
                                 layer=l, batch=batch, seq=ctx_len, tm=ctx_len, shared_mod=True)
    return xs.reshape(batch, seq, d)
```

```python
import functools
import math

import jax
import jax.numpy as jnp
from jax import lax
from jax.experimental import pallas as pl
from jax.experimental.pallas import tpu as pltpu

D_MODEL = 1024
GRID_W = 64
HEAD_DIM = 64
N_Q_HEADS = 8
N_KV_HEADS = 2
GQA_GROUP = N_Q_HEADS // N_KV_HEADS
ATTN_WIDTH = N_Q_HEADS * HEAD_DIM
KV_WIDTH = N_KV_HEADS * HEAD_DIM
CONV_WIDTH = 256
CONV_K = 3
FOURIER_GROUPS = 4
FOURIER_GROUP_DIM = 64
FOURIER_WIDTH = FOURIER_GROUPS * FOURIER_GROUP_DIM
N_BRANCHES = 3
D_FF = 4 * D_MODEL
ROPE_THETA = 10000.0
ROPE_HALF = HEAD_DIM // 2
EPS = 1e-6
N_MOD = 6

OFF_B = 0
OFF_C = OFF_B + CONV_WIDTH
OFF_X = OFF_C + CONV_WIDTH
OFF_F = OFF_X + CONV_WIDTH
OFF_Q = OFF_F + FOURIER_WIDTH
OFF_K = OFF_Q + ATTN_WIDTH
OFF_V = OFF_K + KV_WIDTH
OFF_G = OFF_V + KV_WIDTH

LANES = 128
SUBLANES = 8
BF16_SUBLANES = 16
QK_WIDTH = ATTN_WIDTH + KV_WIDTH
V_ROWS = HEAD_DIM + BF16_SUBLANES
Q_SCALE = HEAD_DIM ** -0.5 * 1.4426950408889634
ATTN_TQ = 256
ATTN_TK = 512
ATTN_Q_TILES = 4
MOD_TN = 1024
FF_CHUNK = 1024
CAST_STEPS = 8
IN_PROJ_TM = 1024
MERGE_TM = 512
IN_PROJ_SUB = 256
MERGE_LAG = 2
MERGE_SUB = 256
MOD_ROWS = 8

BF16 = jnp.bfloat16
F32 = jnp.float32


def _dot(a, b):
    return jnp.dot(a, b, preferred_element_type=F32)


def _tile_bytes(shape, dtype):
    itemsize = jnp.dtype(dtype).itemsize
    dims = [1 if d is None else d for d in shape]
    sublanes = SUBLANES * (4 // itemsize)
    dims[-1] = -(-dims[-1] // LANES) * LANES
    if len(dims) > 1:
        dims[-2] = -(-dims[-2] // sublanes) * sublanes
    return math.prod(dims) * itemsize


def _pallas(kernel, *, name, grid, in_specs, out_specs, out_shape, args, scratch=(), temporaries=()):
    outs = out_shape if isinstance(out_shape, tuple) else (out_shape,)
    ospecs = out_specs if isinstance(out_specs, tuple) else (out_specs,)
    total = 0
    for spec, dtype in [(s, a.dtype) for s, a in zip(in_specs, args)] + [(s, o.dtype) for s, o in zip(ospecs, outs)]:
        buffers = spec.pipeline_mode.buffer_count if spec.pipeline_mode is not None else 2
        total += buffers * _tile_bytes(spec.block_shape, dtype)
    total += sum(_tile_bytes(s.shape, s.dtype) for s in scratch)
    total += sum(_tile_bytes(shape, dtype) for shape, dtype in temporaries)
    return pl.pallas_call(
        kernel, out_shape=out_shape, grid=grid, in_specs=in_specs, out_specs=out_specs, scratch_shapes=list(scratch),
        compiler_params=pltpu.CompilerParams(dimension_semantics=("arbitrary",) * len(grid), vmem_limit_bytes=total),
        name=name,
    )(*args)


def _modulate(x, g, shift, scale):
    ms = jnp.mean(x * x, axis=-1, keepdims=True)
    y = x * lax.rsqrt(ms + EPS)
    return (y * g) * (1.0 + scale) + shift


def _sigmoid(x):
    return 1.0 / (1.0 + jnp.exp(-x))


def _strided_pitch(n):
    tiles = -(-n // SUBLANES)
    return SUBLANES * (tiles if tiles % 2 else tiles + 1)


def _tiled_store(scr, rows, value):
    for j in range(scr.shape[0]):
        scr[j, rows, :] = value[:, j * LANES:(j + 1) * LANES]


def _tiled_load(scr, rows):
    return jnp.concatenate([scr[j, rows, :] for j in range(scr.shape[0])], axis=-1)


def _mod_kernel(c_ref, w_ref, b_ref, o_ref):
    c = c_ref[...]
    s = (c * _sigmoid(c)).astype(BF16)
    o_ref[...] = _dot(s, w_ref[...].astype(BF16)) + b_ref[...]


def _mod_call(cc, w_mod, b_mod):
    depth = w_mod.shape[0]
    tn = MOD_TN
    return _pallas(
        _mod_kernel,
        name="mod",
        out_shape=jax.ShapeDtypeStruct((depth, MOD_ROWS, N_MOD * D_MODEL), F32),
        grid=(depth, N_MOD * D_MODEL // tn),
        in_specs=[
            pl.BlockSpec((MOD_ROWS, D_MODEL), lambda l, j: (0, 0)),
            pl.BlockSpec((None, D_MODEL, tn), lambda l, j: (l, 0, j)),
            pl.BlockSpec((None, 1, tn), lambda l, j: (l, 0, j)),
        ],
        out_specs=pl.BlockSpec((None, MOD_ROWS, tn), lambda l, j: (l, 0, j)),
        args=(cc, w_mod, b_mod.reshape(depth, 1, N_MOD * D_MODEL)),
        temporaries=[((D_MODEL, tn), BF16)],
    )


def _cast_kernel(*refs):
    n = len(refs) // 2
    for src, dst in zip(refs[:n], refs[n:]):
        dst[...] = src[...].astype(BF16)


def _cast_weights(w_in, w_conv_out, w_four_out, w_attn_out, w_o):
    depth = w_in.shape[0]
    ins = (w_in, w_conv_out, w_four_out, w_attn_out, w_o)
    cols = (OFF_G,) + tuple(w.shape[2] for w in ins[1:])
    spec = lambda w, c: pl.BlockSpec((None, w.shape[1] // CAST_STEPS, c), lambda l, i: (l, i, 0))
    return _pallas(
        _cast_kernel,
        name="cast_weights",
        out_shape=tuple(jax.ShapeDtypeStruct((depth, w.shape[1], c), BF16) for w, c in zip(ins, cols)),
        grid=(depth, CAST_STEPS),
        in_specs=[spec(w, c) for w, c in zip(ins, cols)],
        out_specs=tuple(spec(w, c) for w, c in zip(ins, cols)),
        args=ins,
    )


def _in_proj_kernel(x_ref, mod_ref, g1_ref, w_ref, gqk_ref, cos_ref, sin_ref, w64_ref,
                    bz_ref, u_ref, fab_ref, qt_ref, k_ref, vt_ref, *fab_scr, sub):
    tm = x_ref.shape[0]
    pitch = _strided_pitch(GRID_W)
    for r0 in range(0, tm, sub):
        rows = slice(r0, r0 + sub)
        h = _modulate(x_ref[rows, :], g1_ref[...], mod_ref[0:1, :], mod_ref[1:2, :]).astype(BF16)
        z = _dot(h, w_ref[...])
        bz_ref[rows, :] = z[:, OFF_B:OFF_C].astype(BF16)
        u_ref[rows, :] = (z[:, OFF_C:OFF_X] * z[:, OFF_X:OFF_F]).astype(BF16)
        fab = _dot(z[:, OFF_F:OFF_Q].astype(BF16), w64_ref[...])
        if fab_scr:
            for a in range(sub // GRID_W):
                start = (r0 // GRID_W + a) * pitch
                _tiled_store(fab_scr[0], slice(start, start + GRID_W), fab[a * GRID_W:(a + 1) * GRID_W, :])
        else:
            fab_ref[rows, :] = fab.astype(BF16)

        lane = lax.broadcasted_iota(jnp.int32, (sub, LANES), 1)
        head_lo = lane < HEAD_DIM
        half_lo = (lane % HEAD_DIM) < ROPE_HALF
        cos = cos_ref[rows, :]
        sin = sin_ref[rows, :]
        for j in range(QK_WIDTH // LANES):
            zj = z[:, OFF_Q + j * LANES:OFF_Q + (j + 1) * LANES]
            sq = zj * zj
            s_lo = jnp.sum(jnp.where(head_lo, sq, 0.0), axis=-1, keepdims=True)
            s_hi = jnp.sum(jnp.where(head_lo, 0.0, sq), axis=-1, keepdims=True)
            r = jnp.where(head_lo, lax.rsqrt(s_lo / HEAD_DIM + EPS), lax.rsqrt(s_hi / HEAD_DIM + EPS))
            n = (zj * r) * gqk_ref[:, j * LANES:(j + 1) * LANES]
            partner = jnp.where(half_lo, pltpu.roll(n, LANES - ROPE_HALF, 1), pltpu.roll(n, ROPE_HALF, 1))
            rot = n * cos + partner * sin
            if j < ATTN_WIDTH // LANES:
                qt_ref[j * LANES:(j + 1) * LANES, rows] = (rot * Q_SCALE).T.astype(BF16)
            else:
                k_ref[rows, :] = rot.astype(BF16)
        vt = z[:, OFF_V:OFF_G].T.astype(BF16)
        for g in range(N_KV_HEADS):
            vt_ref[g * V_ROWS:g * V_ROWS + HEAD_DIM, rows] = vt[g * HEAD_DIM:(g + 1) * HEAD_DIM, :]
            vt_ref[g * V_ROWS + HEAD_DIM:(g + 1) * V_ROWS, rows] = jnp.ones((V_ROWS - HEAD_DIM, sub), BF16)
    if fab_scr:
        for n2 in range(GRID_W):
            fab_ref[n2] = _tiled_load(fab_scr[0], pl.ds(n2, tm // GRID_W, stride=pitch)).astype(BF16)


def _in_proj_call(x2, mod, g1, w1, gqk, cos_t, sin_t, w64, *, layer, batch, seq, tm, shared_mod):
    n = batch * seq
    tps = seq // tm
    mod_idx = (lambda i: (0, 0, 0)) if shared_mod else (lambda i: (i // tps, 0, 0))
    const = lambda i: (0, 0)
    tile_rows = tm // GRID_W
    column_major = tile_rows >= BF16_SUBLANES
    if column_major:
        fab_shape = jax.ShapeDtypeStruct((batch, GRID_W, seq // GRID_W, 2 * FOURIER_WIDTH), BF16)
        fab_spec = pl.BlockSpec((None, GRID_W, tile_rows, 2 * FOURIER_WIDTH), lambda i: (i // tps, 0, i % tps, 0))
        scratch = [pltpu.VMEM((2 * FOURIER_WIDTH // LANES, tile_rows * _strided_pitch(GRID_W), LANES), F32)]
    else:
        fab_shape = jax.ShapeDtypeStruct((n, 2 * FOURIER_WIDTH), BF16)
        fab_spec = pl.BlockSpec((tm, 2 * FOURIER_WIDTH), lambda i: (i, 0))
        scratch = []
    sub = min(tm, IN_PROJ_SUB)
    return _pallas(
        functools.partial(_in_proj_kernel, sub=sub),
        name="in_proj",
        out_shape=(
            jax.ShapeDtypeStruct((n, CONV_WIDTH), BF16),
            jax.ShapeDtypeStruct((n, CONV_WIDTH), BF16),
            fab_shape,
            jax.ShapeDtypeStruct((batch, ATTN_WIDTH, seq), BF16),
            jax.ShapeDtypeStruct((n, KV_WIDTH), BF16),
            jax.ShapeDtypeStruct((batch, N_KV_HEADS * V_ROWS, seq), BF16),
        ),
        grid=(n // tm,),
        in_specs=[
            pl.BlockSpec((tm, D_MODEL), lambda i: (i, 0)),
            pl.BlockSpec((None, N_MOD, D_MODEL), mod_idx),
            pl.BlockSpec((1, D_MODEL), const),
            pl.BlockSpec((None, D_MODEL, OFF_G), lambda i: (layer, 0, 0)),
            pl.BlockSpec((1, QK_WIDTH), const),
            pl.BlockSpec((tm, LANES), lambda i: (i % tps, 0)),
            pl.BlockSpec((tm, LANES), lambda i: (i % tps, 0)),
            pl.BlockSpec((FOURIER_WIDTH, 2 * FOURIER_WIDTH), const),
        ],
        out_specs=(
            pl.BlockSpec((tm, CONV_WIDTH), lambda i: (i, 0)),
            pl.BlockSpec((tm, CONV_WIDTH), lambda i: (i, 0)),
            fab_spec,
            pl.BlockSpec((None, ATTN_WIDTH, tm), lambda i: (i // tps, 0, i % tps)),
            pl.BlockSpec((tm, KV_WIDTH), lambda i: (i, 0)),
            pl.BlockSpec((None, N_KV_HEADS * V_ROWS, tm), lambda i: (i // tps, 0, i % tps)),
        ),
        args=(x2, mod, g1, w1, gqk, cos_t, sin_t, w64),
        scratch=scratch,
        temporaries=[((sub, OFF_G), F32), ((sub, D_MODEL), F32), ((QK_WIDTH + KV_WIDTH, sub), F32)] * 2,
    )


def _fft_kernel(z_ref, cr_ref, sr_ref, twc_ref, tws_ref, cc_ref, sc_ref, y_ref, g_scr, y_scr):
    w = FOURIER_WIDTH
    rows = z_ref.shape[1]
    pitch = _strided_pitch(rows)
    for n2 in range(GRID_W):
        zb = z_ref[n2]
        p = _dot(cr_ref[...], zb)
        q = _dot(sr_ref[...], zb)
        gr = p[:, :w] + q[:, w:]
        gi = p[:, w:] - q[:, :w]
        tc = twc_ref[n2]
        ts = tws_ref[n2]
        g = jnp.concatenate([gr * tc + gi * ts, gi * tc - gr * ts], axis=-1)
        _tiled_store(g_scr, slice(n2 * pitch, n2 * pitch + rows), g)
    for k1 in range(rows):
        slab = _tiled_load(g_scr, pl.ds(k1, GRID_W, stride=pitch)).astype(BF16)
        y = _dot(cc_ref[...], slab[:, :w]) + _dot(sc_ref[...], slab[:, w:])
        _tiled_store(y_scr, pl.ds(k1, GRID_W, stride=pitch), y)
    for k2 in range(GRID_W):
        y_ref[k2 * rows:(k2 + 1) * rows, :] = _tiled_load(y_scr, slice(k2 * pitch, k2 * pitch + rows)).astype(BF16)


def _fft_call(zt, c_r, s_r, twc, tws, c_c, s_c, *, batch, rows):
    seq = rows * GRID_W
    const2 = lambda b: (0, 0)
    const3 = lambda b: (0, 0, 0)
    pitch_rows = GRID_W * _strided_pitch(rows)
    return _pallas(
        _fft_kernel,
        name="fft",
        out_shape=jax.ShapeDtypeStruct((batch * seq, FOURIER_WIDTH), BF16),
        grid=(batch,),
        in_specs=[
            pl.BlockSpec((None, GRID_W, rows, 2 * FOURIER_WIDTH), lambda b: (b, 0, 0, 0)),
            pl.BlockSpec((rows, rows), const2),
            pl.BlockSpec((rows, rows), const2),
            pl.BlockSpec((GRID_W, rows, 1), const3),
            pl.BlockSpec((GRID_W, rows, 1), const3),
            pl.BlockSpec((GRID_W, GRID_W), const2),
            pl.BlockSpec((GRID_W, GRID_W), const2),
        ],
        out_specs=pl.BlockSpec((seq, FOURIER_WIDTH), lambda b: (b, 0)),
        args=(zt, c_r, s_r, twc, tws, c_c, s_c),
        scratch=[pltpu.VMEM((2 * FOURIER_WIDTH // LANES, pitch_rows, LANES), F32),
                 pltpu.VMEM((FOURIER_WIDTH // LANES, pitch_rows, LANES), F32)],
        temporaries=[((rows, 2 * FOURIER_WIDTH), F32)] * 8,
    )


def _dft_tables(n):
    idx = jnp.arange(n, dtype=jnp.int32)
    ang = (2.0 * jnp.pi / n) * ((idx[:, None] * idx[None, :]) % n).astype(F32)
    return jnp.cos(ang), jnp.sin(ang)


def _dft_dense_kernel(z_ref, c_ref, s_ref, y_ref):
    w = FOURIER_WIDTH
    y_ref[...] = (_dot(c_ref[...], z_ref[:, :w]) + _dot(s_ref[...], z_ref[:, w:])).astype(BF16)


def _dft_dense_call(fab, c_l, s_l, *, batch, seq):
    return _pallas(
        _dft_dense_kernel,
        name="dft_dense",
        out_shape=jax.ShapeDtypeStruct((batch * seq, FOURIER_WIDTH), BF16),
        grid=(batch,),
        in_specs=[
            pl.BlockSpec((seq, 2 * FOURIER_WIDTH), lambda b: (b, 0)),
            pl.BlockSpec((seq, seq), lambda b: (0, 0)),
            pl.BlockSpec((seq, seq), lambda b: (0, 0)),
        ],
        out_specs=pl.BlockSpec((seq, FOURIER_WIDTH), lambda b: (b, 0)),
        args=(fab, c_l, s_l),
        temporaries=[((seq, FOURIER_WIDTH), F32)] * 2,
    )


def _fourier_mix(fab, *, batch, seq):
    rows = seq // GRID_W
    scale = (seq * FOURIER_GROUP_DIM) ** -0.5
    if rows < BF16_SUBLANES:
        c_l, s_l = _dft_tables(seq)
        return _dft_dense_call(fab, (c_l * scale).astype(BF16), (s_l * scale).astype(BF16), batch=batch, seq=seq)
    c_r, s_r = _dft_tables(rows)
    c_c, s_c = _dft_tables(GRID_W)
    n2 = jnp.arange(GRID_W, dtype=jnp.int32)
    k1 = jnp.arange(rows, dtype=jnp.int32)
    tw_ang = (2.0 * jnp.pi / seq) * (n2[:, None] * k1[None, :]).astype(F32)[:, :, None]
    return _fft_call(fab, c_r.astype(BF16), s_r.astype(BF16), jnp.cos(tw_ang), jnp.sin(tw_ang),
                     (c_c * scale).astype(BF16), (s_c * scale).astype(BF16), batch=batch, rows=rows)


def _attn_kernel(*refs, chunks, tq, cast_cols):
    n_src = len(chunks)
    n_cast = len(cast_cols)
    qt_ref = refs[0]
    kv_refs = refs[1:1 + 2 * n_src]
    cast_in = refs[1 + 2 * n_src:1 + 2 * n_src + n_cast]
    o_ref = refs[1 + 2 * n_src + n_cast]
    cast_out = refs[2 + 2 * n_src + n_cast:2 + 2 * n_src + 2 * n_cast]
    s_scr = refs[-2:]
    row0 = pl.multiple_of(jnp.minimum(pl.program_id(2), 0), SUBLANES)
    rows = lambda n: pl.ds(row0, n)
    for src, dst, (c0, c1) in zip(cast_in, cast_out, cast_cols):
        dst[...] = src[:, c0:c1].astype(BF16)
    g = pl.program_id(1)
    row = lax.broadcasted_iota(jnp.int32, (KV_WIDTH, tq), 0)
    own_group = (row >= g * HEAD_DIM) & (row < (g + 1) * HEAD_DIM)
    steps = [(src, c) for src, (n_chunks, _) in enumerate(chunks) for c in range(n_chunks)]

    def k_chunk(src, c):
        tk = chunks[src][1]
        return kv_refs[2 * src][c * tk:(c + 1) * tk, :]

    def v_chunk(src, c):
        tk = chunks[src][1]
        return kv_refs[2 * src + 1][:, c * tk:(c + 1) * tk]

    def scores(qz_u, u, kc, slot):
        n = kc.shape[0]
        s = _dot(kc, qz_u)
        s_scr[slot][u, rows(n), :] = s
        part = jnp.max(s.reshape(n // HEAD_DIM, HEAD_DIM, tq), axis=0)
        return jnp.max(part, axis=0, keepdims=True)

    def accumulate(u, carry_u, chunk_max, vc, slot):
        n = vc.shape[1]
        m, acc = carry_u
        m_new = jnp.maximum(m, chunk_max)
        p = jnp.exp2(s_scr[slot][u, rows(n), :] - m_new).astype(BF16)
        return m_new, jnp.exp2(m - m_new) * acc + _dot(vc, p)

    n_tiles = qt_ref.shape[1] // tq
    chains = [(i, h) for i in range(n_tiles) for h in range(GQA_GROUP)]
    qz = [jnp.where(own_group,
                    jnp.concatenate([qt_ref[h * HEAD_DIM:(h + 1) * HEAD_DIM, i * tq:(i + 1) * tq]] * N_KV_HEADS, axis=0),
                    jnp.zeros((), BF16)) for i, h in chains]
    carry = [(jnp.full((1, tq), -jnp.inf, F32), jnp.zeros((V_ROWS, tq), F32)) for _ in chains]
    first_chunk = k_chunk(*steps[0])
    maxima = [scores(qz[u], u, first_chunk, 0) for u in range(len(chains))]
    for t, cur in enumerate(steps):
        slot = t % 2
        vc = v_chunk(*cur)
        kc = k_chunk(*steps[t + 1]) if t + 1 < len(steps) else None
        next_maxima = []
        for u in range(len(chains)):
            if kc is not None:
                next_maxima.append(scores(qz[u], u, kc, 1 - slot))
            carry[u] = accumulate(u, carry[u], maxima[u], vc, slot)
        maxima = next_maxima
    for i in range(n_tiles):
        outs = [acc[:HEAD_DIM, :] / acc[HEAD_DIM:HEAD_DIM + 1, :] for _, acc in carry[i * GQA_GROUP:(i + 1) * GQA_GROUP]]
        o_ref[i * tq:(i + 1) * tq, :] = jnp.concatenate(outs, axis=0).T.astype(BF16)


def _attn_call(qt, sources, *, batch, seq, tq, tk, q_tiles, cast=(), layer=0):
    bq = tq * q_tiles
    nq = seq // bq
    gw = GQA_GROUP * HEAD_DIM
    n_steps = batch * N_KV_HEADS * nq
    step = lambda b, g, i: (b * N_KV_HEADS + g) * nq + i
    in_specs = [pl.BlockSpec((None, gw, bq), lambda b, g, i: (b, g, i))]
    args = [qt]
    chunks = []
    for k, vt in sources:
        ln = k.shape[1]
        ck = min(tk, ln)
        chunks.append((ln // ck, ck))
        in_specs.append(pl.BlockSpec((None, ln, KV_WIDTH), lambda b, g, i: (b, 0, 0)))
        in_specs.append(pl.BlockSpec((None, V_ROWS, ln), lambda b, g, i: (b, g, 0)))
        args += [k, vt]
    out_shape = [jax.ShapeDtypeStruct((batch * seq, ATTN_WIDTH), BF16)]
    out_specs = [pl.BlockSpec((bq, gw), lambda b, g, i: (b * nq + i, g))]
    for w, c0, c1 in cast:
        rows = w.shape[1] // n_steps
        assert rows * n_steps == w.shape[1] and rows % BF16_SUBLANES == 0, w.shape
        in_specs.append(pl.BlockSpec((None, rows, w.shape[2]), lambda b, g, i: (layer, step(b, g, i), 0)))
        args.append(w)
        out_shape.append(jax.ShapeDtypeStruct((w.shape[1], c1 - c0), BF16))
        out_specs.append(pl.BlockSpec((rows, c1 - c0), lambda b, g, i: (step(b, g, i), 0)))
    max_chunk = max(ck for _, ck in chunks)
    return _pallas(
        functools.partial(_attn_kernel, chunks=tuple(chunks), tq=tq, cast_cols=tuple((c0, c1) for _, c0, c1 in cast)),
        name="attention",
        out_shape=tuple(out_shape),
        grid=(batch, N_KV_HEADS, nq),
        in_specs=in_specs,
        out_specs=tuple(out_specs),
        args=args,
        scratch=[pltpu.VMEM((q_tiles * GQA_GROUP, max_chunk, tq), F32)] * 2,
        temporaries=[((max_chunk, tq), F32), ((max_chunk, tq), BF16), ((V_ROWS, tq), F32)] * (q_tiles * GQA_GROUP),
    )


def _merge_mlp_kernel(x_ref, mod_ref, g1_ref, g2_ref, wg_ref, bz_ref, u_ref, up_ref, un_ref, wconv_ref,
                      yf_ref, ya_ref, wc_ref, wf_ref, wa_ref, wo_ref, w1_ref, w2_ref, o_ref, *, tps, sub, ff_chunk):
    tm = x_ref.shape[0]

    pos = pl.program_id(0) % tps
    u = u_ref[...].astype(F32)
    prev_row = jnp.where(pos == 0, 0.0, up_ref[...].astype(F32)[BF16_SUBLANES - 1:BF16_SUBLANES, :])
    next_row = jnp.where(pos == tps - 1, 0.0, un_ref[...].astype(F32)[0:1, :])
    row = lax.broadcasted_iota(jnp.int32, u.shape, 0)
    u_m1 = jnp.where(row == 0, prev_row, pltpu.roll(u, 1, 0))
    u_p1 = jnp.where(row == tm - 1, next_row, pltpu.roll(u, tm - 1, 0))
    conv = wconv_ref[0:1, :] * u_m1 + wconv_ref[1:2, :] * u + wconv_ref[2:3, :] * u_p1
    y_conv = (bz_ref[...].astype(F32) * conv).astype(BF16)

    d = D_MODEL

    def chain(r0):
        rows = slice(r0, r0 + sub)
        x = x_ref[rows, :]
        h = _modulate(x, g1_ref[...], mod_ref[0:1, :], mod_ref[1:2, :]).astype(BF16)
        yield
        m = _sigmoid(_dot(h, wg_ref[:, 0:d])) * _dot(y_conv[rows, :], wc_ref[...])
        yield
        m = m + _sigmoid(_dot(h, wg_ref[:, d:2 * d])) * _dot(yf_ref[rows, :], wf_ref[...])
        yield
        m = m + _sigmoid(_dot(h, wg_ref[:, 2 * d:3 * d])) * _dot(ya_ref[rows, :], wa_ref[...])
        yield
        x1 = x + mod_ref[2:3, :] * _dot(m.astype(BF16), wo_ref[...])
        h2 = _modulate(x1, g2_ref[...], mod_ref[3:4, :], mod_ref[4:5, :]).astype(BF16)
        acc = jnp.zeros(x1.shape, F32)
        for c in range(D_FF // ff_chunk):
            yield
            a = jnp.maximum(_dot(h2, w1_ref[:, c * ff_chunk:(c + 1) * ff_chunk]), 0.0)
            acc = acc + _dot((a * a).astype(BF16), w2_ref[c * ff_chunk:(c + 1) * ff_chunk, :])
        o_ref[rows, :] = x1 + mod_ref[5:6, :] * acc

    chains = [chain(r0) for r0 in range(0, tm, sub)]
    live = set(range(len(chains)))
    tick = 0
    while live:
        for i in sorted(live):
            if tick >= i * MERGE_LAG and next(chains[i], "done") == "done":
                live.discard(i)
        tick += 1


def _merge_mlp_call(x2, mod, g1, g2, wg, bz, u, wconv, yf, ya, wc, wf, wa, wo, w1, w2, *,
                    layer, batch, seq, tm, shared_mod):
    n = batch * seq
    tps = seq // tm
    hb = tm // BF16_SUBLANES
    n_hb = n // BF16_SUBLANES
    mod_idx = (lambda i: (0, 0, 0)) if shared_mod else (lambda i: (i // tps, 0, 0))
    const = lambda i: (0, 0)
    tile = lambda w: pl.BlockSpec((tm, w), lambda i: (i, 0))
    def weight(w):
        if w.ndim == 2:
            return pl.BlockSpec(w.shape, const, pipeline_mode=pl.Buffered(1))
        return pl.BlockSpec((None,) + w.shape[1:], lambda i: (layer, 0, 0), pipeline_mode=pl.Buffered(1))
    sub = min(tm, MERGE_SUB)
    return _pallas(
        functools.partial(_merge_mlp_kernel, tps=tps, sub=sub, ff_chunk=FF_CHUNK),
        name="merge_mlp",
        out_shape=jax.ShapeDtypeStruct((n, D_MODEL), F32),
        grid=(n // tm,),
        in_specs=[
            tile(D_MODEL),
            pl.BlockSpec((None, N_MOD, D_MODEL), mod_idx),
            pl.BlockSpec((1, D_MODEL), const),
            pl.BlockSpec((1, D_MODEL), const),
            weight(wg),
            tile(CONV_WIDTH),
            tile(CONV_WIDTH),
            pl.BlockSpec((BF16_SUBLANES, CONV_WIDTH), lambda i: (jnp.maximum(i * hb - 1, 0), 0)),
            pl.BlockSpec((BF16_SUBLANES, CONV_WIDTH), lambda i: (jnp.minimum((i + 1) * hb, n_hb - 1), 0)),
            pl.BlockSpec((CONV_K, CONV_WIDTH), const),
            tile(FOURIER_WIDTH),
            tile(ATTN_WIDTH),
            weight(wc),
            weight(wf),
            weight(wa),
            weight(wo),
            weight(w1),
            weight(w2),
        ],
        out_specs=tile(D_MODEL),
        args=(x2, mod, g1, g2, wg, bz, u, u, u, wconv, yf, ya, wc, wf, wa, wo, w1, w2),
        temporaries=[((tm, CONV_WIDTH), F32)] * 4
        + ([((sub, D_MODEL), F32)] * 8 + [((sub, FF_CHUNK), F32), ((sub, FF_CHUNK), BF16)]) * (tm // sub),
    )


def _rope_tables(rows):
    n_freq = ROPE_HALF // 2
    inv = ROPE_THETA ** (-jnp.arange(n_freq, dtype=F32) / n_freq)
    row_ang = jnp.arange(rows, dtype=F32)[:, None] * inv
    col_ang = jnp.arange(GRID_W, dtype=F32)[:, None] * inv
    expand = lambda r, c: jnp.concatenate([jnp.repeat(r, GRID_W, axis=0), jnp.tile(c, (rows, 1))], axis=-1)
    cos = expand(jnp.cos(row_ang), jnp.cos(col_ang))
    sin = expand(jnp.sin(row_ang), jnp.sin(col_ang))
    cos_h = jnp.concatenate([cos, cos], axis=-1)
    sin_h = jnp.concatenate([-sin, sin], axis=-1)
    reps = LANES // HEAD_DIM
    return jnp.tile(cos_h, (1, reps)), jnp.tile(sin_h, (1, reps))


def _channel_dft_matrix():
    c, s = _dft_tables(FOURIER_GROUP_DIM)
    eye = jnp.eye(FOURIER_GROUPS, dtype=F32)
    return jnp.concatenate([jnp.kron(eye, c), jnp.kron(eye, -s)], axis=1).astype(BF16)


def kernel(x, c, ctx, c_ctx, w_mod, b_mod, g_norm1, g_norm2, w_in, w_conv, g_q, g_k,
           w_conv_out, w_four_out, w_attn_out, w_o, w_ff1, w_ff2):
    batch, seq, d = x.shape
    ctx_len = ctx.shape[1]
    depth = w_mod.shape[0]
    rows = seq // GRID_W

    cc = jnp.zeros((MOD_ROWS, d), F32).at[:batch].set(c).at[batch].set(c_ctx)
    mods = _mod_call(cc, w_mod, b_mod)

    cos_t, sin_t = _rope_tables(rows)
    cos_c = jnp.ones((ctx_len, LANES), F32)
    sin_c = jnp.zeros((ctx_len, LANES), F32)
    w64 = _channel_dft_matrix()
    w1, wc, wf, wa, wo = _cast_weights(w_in, w_conv_out, w_four_out, w_attn_out, w_o)
    big_weights = ((w_in, OFF_G, w_in.shape[2]), (w_ff1, 0, w_ff1.shape[2]), (w_ff2, 0, w_ff2.shape[2]))

    xs = x.reshape(batch * seq, d)
    cs = ctx.reshape(batch * ctx_len, d)
    for l in range(depth):
        mod_x = mods[l, :batch].reshape(batch, N_MOD, d)
        mod_c = mods[l, batch:batch + 1].reshape(1, N_MOD, d)
        g1 = g_norm1[l].reshape(1, d)
        g2 = g_norm2[l].reshape(1, d)
        gqk = jnp.concatenate([jnp.tile(g_q[l], N_Q_HEADS), jnp.tile(g_k[l], N_KV_HEADS)]).reshape(1, QK_WIDTH)
        last = l == depth - 1

        bz_c, u_c, fab_c, qt_c, k_c, vt_c = _in_proj_call(
            cs, mod_c, g1, w1, gqk, cos_c, sin_c, w64,
            layer=l, batch=batch, seq=ctx_len, tm=ctx_len, shared_mod=True)
        k_c3 = k_c.reshape(batch, ctx_len, KV_WIDTH)

        bz, u, fab, qt, k, vt = _in_proj_call(
            xs, mod_x, g1, w1, gqk, cos_t, sin_t, w64,
            layer=l, batch=batch, seq=seq, tm=IN_PROJ_TM, shared_mod=False)
        ya, wg, wf1, wf2 = _attn_call(qt, [(k.reshape(batch, seq, KV_WIDTH), vt), (k_c3, vt_c)],
                                      batch=batch, seq=seq, tq=ATTN_TQ, tk=ATTN_TK, q_tiles=ATTN_Q_TILES,
                                      cast=big_weights, layer=l)
        yf = _fourier_mix(fab, batch=batch, seq=seq)
        xs = _merge_mlp_call(xs, mod_x, g1, g2, wg, bz, u, w_conv[l], yf, ya, wc, wf, wa, wo, wf1, wf2,
                             layer=l, batch=batch, seq=seq, tm=MERGE_TM, shared_mod=False)

        if not last:
            (ya_c,) = _attn_call(qt_c, [(k_c3, vt_c)], batch=batch, seq=ctx_len, tq=ATTN_TQ, tk=ATTN_TK,
                                 q_tiles=ctx_len // ATTN_TQ)
            yf_c = _fourier_mix(fab_c, batch=batch, seq=ctx_len)
            cs = _merge_mlp_call(cs, mod_c, g1, g2, wg, bz_c, u_c, w_conv[l], yf_c, ya_c, wc, wf, wa, wo, wf1, wf2,
                                 layer=l, batch=batch, seq=ctx_len, tm=ctx_len, shared_mod=True)
    return xs.reshape(batch, seq, d)
```

```python
import functools
import math

import jax
import jax.numpy as jnp
from jax import lax
from jax.experimental import pallas as pl
from jax.experimental.pallas import tpu as pltpu

D_MODEL = 1024
GRID_W = 64
HEAD_DIM = 64
N_Q_HEADS = 8
N_KV_HEADS = 2
GQA_GROUP = N_Q_HEADS // N_KV_HEADS
ATTN_WIDTH = N_Q_HEADS * HEAD_DIM
KV_WIDTH = N_KV_HEADS * HEAD_DIM
CONV_WIDTH = 256
CONV_K = 3
FOURIER_GROUPS = 4
FOURIER_GROUP_DIM = 64
FOURIER_WIDTH = FOURIER_GROUPS * FOURIER_GROUP_DIM
N_BRANCHES = 3
D_FF = 4 * D_MODEL
ROPE_THETA = 10000.0
ROPE_HALF = HEAD_DIM // 2
EPS = 1e-6
N_MOD = 6

OFF_B = 0
OFF_C = OFF_B + CONV_WIDTH
OFF_X = OFF_C + CONV_WIDTH
OFF_F = OFF_X + CONV_WIDTH
OFF_Q = OFF_F + FOURIER_WIDTH
OFF_K = OFF_Q + ATTN_WIDTH
OFF_V = OFF_K + KV_WIDTH
OFF_G = OFF_V + KV_WIDTH

LANES = 128
SUBLANES = 8
BF16_SUBLANES = 16
QK_WIDTH = ATTN_WIDTH + KV_WIDTH
V_ROWS = HEAD_DIM + BF16_SUBLANES
Q_SCALE = HEAD_DIM ** -0.5 * math.log2(math.e)
ATTN_TQ = 256
ATTN_TK = 512
ATTN_Q_TILES = 4
MOD_TN = 1024
FF_CHUNK = 1024
CAST_STEPS = 8
IN_PROJ_TM = 1024
MERGE_TM = 512
IN_PROJ_SUB = 256
MERGE_LAG = 2
MERGE_SUB = 256
MOD_ROWS = 8

BF16 = jnp.bfloat16
F32 = jnp.float32


def _dot(a, b):
    return jnp.dot(a, b, preferred_element_type=F32)


def _tile_bytes(shape, dtype):
    itemsize = jnp.dtype(dtype).itemsize
    dims = [1 if d is None else d for d in shape]
    sublanes = SUBLANES * (4 // itemsize)
    dims[-1] = -(-dims[-1] // LANES) * LANES
    if len(dims) > 1:
        dims[-2] = -(-dims[-2] // sublanes) * sublanes
    return math.prod(dims) * itemsize


def _pallas(kernel, *, name, grid, in_specs, out_specs, out_shape, args, scratch=(), temporaries=()):
    outs = out_shape if isinstance(out_shape, tuple) else (out_shape,)
    ospecs = out_specs if isinstance(out_specs, tuple) else (out_specs,)
    total = 0
    for spec, dtype in [(s, a.dtype) for s, a in zip(in_specs, args)] + [(s, o.dtype) for s, o in zip(ospecs, outs)]:
        buffers = spec.pipeline_mode.buffer_count if spec.pipeline_mode is not None else 2
        total += buffers * _tile_bytes(spec.block_shape, dtype)
    total += sum(_tile_bytes(s.shape, s.dtype) for s in scratch)
    total += sum(_tile_bytes(shape, dtype) for shape, dtype in temporaries)
    return pl.pallas_call(
        kernel, out_shape=out_shape, grid=grid, in_specs=in_specs, out_specs=out_specs, scratch_shapes=list(scratch),
        compiler_params=pltpu.CompilerParams(dimension_semantics=("arbitrary",) * len(grid), vmem_limit_bytes=total),
        name=name,
    )(*args)


def _modulate(x, g, shift, scale):
    ms = jnp.mean(x * x, axis=-1, keepdims=True)
    y = x * lax.rsqrt(ms + EPS)
    return (y * g) * (1.0 + scale) + shift


def _sigmoid(x):
    return 1.0 / (1.0 + jnp.exp(-x))


def _strided_pitch(n):
    tiles = -(-n // SUBLANES)
    return SUBLANES * (tiles if tiles % 2 else tiles + 1)


def _tiled_store(scr, rows, value):
    for j in range(scr.shape[0]):
        scr[j, rows, :] = value[:, j * LANES:(j + 1) * LANES]


def _tiled_load(scr, rows):
    return jnp.concatenate([scr[j, rows, :] for j in range(scr.shape[0])], axis=-1)


def _mod_kernel(c_ref, w_ref, b_ref, o_ref):
    c = c_ref[...]
    s = (c * _sigmoid(c)).astype(BF16)
    o_ref[...] = _dot(s, w_ref[...].astype(BF16)) + b_ref[...]


def _mod_call(cc, w_mod, b_mod):
    depth = w_mod.shape[0]
    tn = MOD_TN
    return _pallas(
        _mod_kernel,
        name="mod",
        out_shape=jax.ShapeDtypeStruct((depth, MOD_ROWS, N_MOD * D_MODEL), F32),
        grid=(depth, N_MOD * D_MODEL // tn),
        in_specs=[
            pl.BlockSpec((MOD_ROWS, D_MODEL), lambda l, j: (0, 0)),
            pl.BlockSpec((None, D_MODEL, tn), lambda l, j: (l, 0, j)),
            pl.BlockSpec((None, 1, tn), lambda l, j: (l, 0, j)),
        ],
        out_specs=pl.BlockSpec((None, MOD_ROWS, tn), lambda l, j: (l, 0, j)),
        args=(cc, w_mod, b_mod.reshape(depth, 1, N_MOD * D_MODEL)),
        temporaries=[((D_MODEL, tn), BF16)],
    )


def _cast_kernel(*refs):
    n = len(refs) // 2
    for src, dst in zip(refs[:n], refs[n:]):
        dst[...] = src[...].astype(BF16)


def _cast_weights(w_in, w_conv_out, w_four_out, w_attn_out, w_o):
    depth = w_in.shape[0]
    ins = (w_in, w_conv_out, w_four_out, w_attn_out, w_o)
    cols = (OFF_G,) + tuple(w.shape[2] for w in ins[1:])
    spec = lambda w, c: pl.BlockSpec((None, w.shape[1] // CAST_STEPS, c), lambda l, i: (l, i, 0))
    return _pallas(
        _cast_kernel,
        name="cast_weights",
        out_shape=tuple(jax.ShapeDtypeStruct((depth, w.shape[1], c), BF16) for w, c in zip(ins, cols)),
        grid=(depth, CAST_STEPS),
        in_specs=[spec(w, c) for w, c in zip(ins, cols)],
        out_specs=tuple(spec(w, c) for w, c in zip(ins, cols)),
        args=ins,
    )


def _in_proj_kernel(x_ref, mod_ref, g1_ref, w_ref, gqk_ref, cos_ref, sin_ref, w64_ref,
                    bz_ref, u_ref, fab_ref, qt_ref, k_ref, vt_ref, *fab_scr, sub):
    tm = x_ref.shape[0]
    pitch = _strided_pitch(GRID_W)
    for r0 in range(0, tm, sub):
        rows = slice(r0, r0 + sub)
        h = _modulate(x_ref[rows, :], g1_ref[...], mod_ref[0:1, :], mod_ref[1:2, :]).astype(BF16)
        z = _dot(h, w_ref[...])
        bz_ref[rows, :] = z[:, OFF_B:OFF_C].astype(BF16)
        u_ref[rows, :] = (z[:, OFF_C:OFF_X] * z[:, OFF_X:OFF_F]).astype(BF16)
        fab = _dot(z[:, OFF_F:OFF_Q].astype(BF16), w64_ref[...])
        if fab_scr:
            for a in range(sub // GRID_W):
                start = (r0 // GRID_W + a) * pitch
                _tiled_store(fab_scr[0], slice(start, start + GRID_W), fab[a * GRID_W:(a + 1) * GRID_W, :])
        else:
            fab_ref[rows, :] = fab.astype(BF16)

        lane = lax.broadcasted_iota(jnp.int32, (sub, LANES), 1)
        head_lo = lane < HEAD_DIM
        half_lo = (lane % HEAD_DIM) < ROPE_HALF
        cos = cos_ref[rows, :]
        sin = sin_ref[rows, :]
        for j in range(QK_WIDTH // LANES):
            zj = z[:, OFF_Q + j * LANES:OFF_Q + (j + 1) * LANES]
            sq = zj * zj
            s_lo = jnp.sum(jnp.where(head_lo, sq, 0.0), axis=-1, keepdims=True)
            s_hi = jnp.sum(jnp.where(head_lo, 0.0, sq), axis=-1, keepdims=True)
            r = jnp.where(head_lo, lax.rsqrt(s_lo / HEAD_DIM + EPS), lax.rsqrt(s_hi / HEAD_DIM + EPS))
            n = (zj * r) * gqk_ref[:, j * LANES:(j + 1) * LANES]
            partner = jnp.where(half_lo, pltpu.roll(n, LANES - ROPE_HALF, 1), pltpu.roll(n, ROPE_HALF, 1))
            rot = n * cos + partner * sin
            if j < ATTN_WIDTH // LANES:
                qt_ref[j * LANES:(j + 1) * LANES, rows] = (rot * Q_SCALE).T.astype(BF16)
            else:
                k_ref[rows, :] = rot.astype(BF16)
        vt = z[:, OFF_V:OFF_G].T.astype(BF16)
        for g in range(N_KV_HEADS):
            vt_ref[g * V_ROWS:g * V_ROWS + HEAD_DIM, rows] = vt[g * HEAD_DIM:(g + 1) * HEAD_DIM, :]
            vt_ref[g * V_ROWS + HEAD_DIM:(g + 1) * V_ROWS, rows] = jnp.ones((V_ROWS - HEAD_DIM, sub), BF16)
    if fab_scr:
        for n2 in range(GRID_W):
            fab_ref[n2] = _tiled_load(fab_scr[0], pl.ds(n2, tm // GRID_W, stride=pitch)).astype(BF16)


def _in_proj_call(x2, mod, g1, w1, gqk, cos_t, sin_t, w64, *, layer, batch, seq, tm, shared_mod):
    n = batch * seq
    tps = seq // tm
    mod_idx = (lambda i: (0, 0, 0)) if shared_mod else (lambda i: (i // tps, 0, 0))
    const = lambda i: (0, 0)
    tile_rows = tm // GRID_W
    column_major = tile_rows >= BF16_SUBLANES
    if column_major:
        fab_shape = jax.ShapeDtypeStruct((batch, GRID_W, seq // GRID_W, 2 * FOURIER_WIDTH), BF16)
        fab_spec = pl.BlockSpec((None, GRID_W, tile_rows, 2 * FOURIER_WIDTH), lambda i: (i // tps, 0, i % tps, 0))
        scratch = [pltpu.VMEM((2 * FOURIER_WIDTH // LANES, tile_rows * _strided_pitch(GRID_W), LANES), F32)]
    else:
        fab_shape = jax.ShapeDtypeStruct((n, 2 * FOURIER_WIDTH), BF16)
        fab_spec = pl.BlockSpec((tm, 2 * FOURIER_WIDTH), lambda i: (i, 0))
        scratch = []
    sub = min(tm, IN_PROJ_SUB)
    return _pallas(
        functools.partial(_in_proj_kernel, sub=sub),
        name="in_proj",
        out_shape=(
            jax.ShapeDtypeStruct((n, CONV_WIDTH), BF16),
            jax.ShapeDtypeStruct((n, CONV_WIDTH), BF16),
            fab_shape,
            jax.ShapeDtypeStruct((batch, ATTN_WIDTH, seq), BF16),
            jax.ShapeDtypeStruct((n, KV_WIDTH), BF16),
            jax.ShapeDtypeStruct((batch, N_KV_HEADS * V_ROWS, seq), BF16),
        ),
        grid=(n // tm,),
        in_specs=[
            pl.BlockSpec((tm, D_MODEL), lambda i: (i, 0)),
            pl.BlockSpec((None, N_MOD, D_MODEL), mod_idx),
            pl.BlockSpec((1, D_MODEL), const),
            pl.BlockSpec((None, D_MODEL, OFF_G), lambda i: (layer, 0, 0)),
            pl.BlockSpec((1, QK_WIDTH), const),
            pl.BlockSpec((tm, LANES), lambda i: (i % tps, 0)),
            pl.BlockSpec((tm, LANES), lambda i: (i % tps, 0)),
            pl.BlockSpec((FOURIER_WIDTH, 2 * FOURIER_WIDTH), const),
        ],
        out_specs=(
            pl.BlockSpec((tm, CONV_WIDTH), lambda i: (i, 0)),
            pl.BlockSpec((tm, CONV_WIDTH), lambda i: (i, 0)),
            fab_spec,
            pl.BlockSpec((None, ATTN_WIDTH, tm), lambda i: (i // tps, 0, i % tps)),
            pl.BlockSpec((tm, KV_WIDTH), lambda i: (i, 0)),
            pl.BlockSpec((None, N_KV_HEADS * V_ROWS, tm), lambda i: (i // tps, 0, i % tps)),
        ),
        args=(x2, mod, g1, w1, gqk, cos_t, sin_t, w64),
        scratch=scratch,
        temporaries=[((sub, OFF_G), F32), ((sub, D_MODEL), F32), ((QK_WIDTH + KV_WIDTH, sub), F32)] * 2,
    )


def _fft_kernel(z_ref, cr_ref, sr_ref, twc_ref, tws_ref, cc_ref, sc_ref, y_ref, g_scr, y_scr):
    w = FOURIER_WIDTH
    rows = z_ref.shape[1]
    pitch = _strided_pitch(rows)
    for n2 in range(GRID_W):
        zb = z_ref[n2]
        p = _dot(cr_ref[...], zb)
        q = _dot(sr_ref[...], zb)
        gr = p[:, :w] + q[:, w:]
        gi = p[:, w:] - q[:, :w]
        tc = twc_ref[n2]
        ts = tws_ref[n2]
        g = jnp.concatenate([gr * tc + gi * ts, gi * tc - gr * ts], axis=-1)
        _tiled_store(g_scr, slice(n2 * pitch, n2 * pitch + rows), g)
    for k1 in range(rows):
        slab = _tiled_load(g_scr, pl.ds(k1, GRID_W, stride=pitch)).astype(BF16)
        y = _dot(cc_ref[...], slab[:, :w]) + _dot(sc_ref[...], slab[:, w:])
        _tiled_store(y_scr, pl.ds(k1, GRID_W, stride=pitch), y)
    for k2 in range(GRID_W):
        y_ref[k2 * rows:(k2 + 1) * rows, :] = _tiled_load(y_scr, slice(k2 * pitch, k2 * pitch + rows)).astype(BF16)


def _fft_call(zt, c_r, s_r, twc, tws, c_c, s_c, *, batch, rows):
    seq = rows * GRID_W
    const2 = lambda b: (0, 0)
    const3 = lambda b: (0, 0, 0)
    pitch_rows = GRID_W * _strided_pitch(rows)
    return _pallas(
        _fft_kernel,
        name="fft",
        out_shape=jax.ShapeDtypeStruct((batch * seq, FOURIER_WIDTH), BF16),
        grid=(batch,),
        in_specs=[
            pl.BlockSpec((None, GRID_W, rows, 2 * FOURIER_WIDTH), lambda b: (b, 0, 0, 0)),
            pl.BlockSpec((rows, rows), const2),
            pl.BlockSpec((rows, rows), const2),
            pl.BlockSpec((GRID_W, rows, 1), const3),
            pl.BlockSpec((GRID_W, rows, 1), const3),
            pl.BlockSpec((GRID_W, GRID_W), const2),
            pl.BlockSpec((GRID_W, GRID_W), const2),
        ],
        out_specs=pl.BlockSpec((seq, FOURIER_WIDTH), lambda b: (b, 0)),
        args=(zt, c_r, s_r, twc, tws, c_c, s_c),
        scratch=[pltpu.VMEM((2 * FOURIER_WIDTH // LANES, pitch_rows, LANES), F32),
                 pltpu.VMEM((FOURIER_WIDTH // LANES, pitch_rows, LANES), F32)],
        temporaries=[((rows, 2 * FOURIER_WIDTH), F32)] * 8,
    )


def _dft_tables(n):
    idx = jnp.arange(n, dtype=jnp.int32)
    ang = (2.0 * jnp.pi / n) * ((idx[:, None] * idx[None, :]) % n).astype(F32)
    return jnp.cos(ang), jnp.sin(ang)


def _dft_dense_kernel(z_ref, c_ref, s_ref, y_ref):
    w = FOURIER_WIDTH
    y_ref[...] = (_dot(c_ref[...], z_ref[:, :w]) + _dot(s_ref[...], z_ref[:, w:])).astype(BF16)


def _dft_dense_call(fab, c_l, s_l, *, batch, seq):
    return _pallas(
        _dft_dense_kernel,
        name="dft_dense",
        out_shape=jax.ShapeDtypeStruct((batch * seq, FOURIER_WIDTH), BF16),
        grid=(batch,),
        in_specs=[
            pl.BlockSpec((seq, 2 * FOURIER_WIDTH), lambda b: (b, 0)),
            pl.BlockSpec((seq, seq), lambda b: (0, 0)),
            pl.BlockSpec((seq, seq), lambda b: (0, 0)),
        ],
        out_specs=pl.BlockSpec((seq, FOURIER_WIDTH), lambda b: (b, 0)),
        args=(fab, c_l, s_l),
        temporaries=[((seq, FOURIER_WIDTH), F32)] * 2,
    )


def _fourier_mix(fab, *, batch, seq):
    rows = seq // GRID_W
    scale = (seq * FOURIER_GROUP_DIM) ** -0.5
    if rows < BF16_SUBLANES:
        c_l, s_l = _dft_tables(seq)
        return _dft_dense_call(fab, (c_l * scale).astype(BF16), (s_l * scale).astype(BF16), batch=batch, seq=seq)
    c_r, s_r = _dft_tables(rows)
    c_c, s_c = _dft_tables(GRID_W)
    n2 = jnp.arange(GRID_W, dtype=jnp.int32)
    k1 = jnp.arange(rows, dtype=jnp.int32)
    tw_ang = (2.0 * jnp.pi / seq) * (n2[:, None] * k1[None, :]).astype(F32)[:, :, None]
    return _fft_call(fab, c_r.astype(BF16), s_r.astype(BF16), jnp.cos(tw_ang), jnp.sin(tw_ang),
                     (c_c * scale).astype(BF16), (s_c * scale).astype(BF16), batch=batch, rows=rows)


def _attn_kernel(*refs, chunks, tq, cast_cols):
    n_src = len(chunks)
    n_cast = len(cast_cols)
    qt_ref = refs[0]
    kv_refs = refs[1:1 + 2 * n_src]
    cast_in = refs[1 + 2 * n_src:1 + 2 * n_src + n_cast]
    o_ref = refs[1 + 2 * n_src + n_cast]
    cast_out = refs[2 + 2 * n_src + n_cast:2 + 2 * n_src + 2 * n_cast]
    s_scr = refs[-2:]
    row0 = pl.multiple_of(jnp.minimum(pl.program_id(2), 0), SUBLANES)
    rows = lambda n: pl.ds(row0, n)
    for src, dst, (c0, c1) in zip(cast_in, cast_out, cast_cols):
        dst[...] = src[:, c0:c1].astype(BF16)
    g = pl.program_id(1)
    row = lax.broadcasted_iota(jnp.int32, (KV_WIDTH, tq), 0)
    own_group = (row >= g * HEAD_DIM) & (row < (g + 1) * HEAD_DIM)
    steps = [(src, c) for src, (n_chunks, _) in enumerate(chunks) for c in range(n_chunks)]

    def k_chunk(src, c):
        tk = chunks[src][1]
        return kv_refs[2 * src][c * tk:(c + 1) * tk, :]

    def v_chunk(src, c):
        tk = chunks[src][1]
        return kv_refs[2 * src + 1][:, c * tk:(c + 1) * tk]

    def scores(qz_u, u, kc, slot):
        n = kc.shape[0]
        s = _dot(kc, qz_u)
        s_scr[slot][u, rows(n), :] = s
        part = jnp.max(s.reshape(n // HEAD_DIM, HEAD_DIM, tq), axis=0)
        return jnp.max(part, axis=0, keepdims=True)

    def accumulate(u, carry_u, chunk_max, vc, slot):
        n = vc.shape[1]
        m, acc = carry_u
        m_new = jnp.maximum(m, chunk_max)
        p = jnp.exp2(s_scr[slot][u, rows(n), :] - m_new).astype(BF16)
        return m_new, jnp.exp2(m - m_new) * acc + _dot(vc, p)

    n_tiles = qt_ref.shape[1] // tq
    chains = [(i, h) for i in range(n_tiles) for h in range(GQA_GROUP)]
    qz = [jnp.where(own_group,
                    jnp.concatenate([qt_ref[h * HEAD_DIM:(h + 1) * HEAD_DIM, i * tq:(i + 1) * tq]] * N_KV_HEADS, axis=0),
                    jnp.zeros((), BF16)) for i, h in chains]
    carry = [(jnp.full((1, tq), -jnp.inf, F32), jnp.zeros((V_ROWS, tq), F32)) for _ in chains]
    first_chunk = k_chunk(*steps[0])
    maxima = [scores(qz[u], u, first_chunk, 0) for u in range(len(chains))]
    for t, cur in enumerate(steps):
        slot = t % 2
        vc = v_chunk(*cur)
        kc = k_chunk(*steps[t + 1]) if t + 1 < len(steps) else None
        next_maxima = []
        for u in range(len(chains)):
            if kc is not None:
                next_maxima.append(scores(qz[u], u, kc, 1 - slot))
            carry[u] = accumulate(u, carry[u], maxima[u], vc, slot)
        maxima = next_maxima
    for i in range(n_tiles):
        outs = [acc[:HEAD_DIM, :] / acc[HEAD_DIM:HEAD_DIM + 1, :] for _, acc in carry[i * GQA_GROUP:(i + 1) * GQA_GROUP]]
        o_ref[i * tq:(i + 1) * tq, :] = jnp.concatenate(outs, axis=0).T.astype(BF16)


def _attn_call(qt, sources, *, batch, seq, tq, tk, q_tiles, cast=(), layer=0):
    bq = tq * q_tiles
    nq = seq // bq
    gw = GQA_GROUP * HEAD_DIM
    n_steps = batch * N_KV_HEADS * nq
    step = lambda b, g, i: (b * N_KV_HEADS + g) * nq + i
    in_specs = [pl.BlockSpec((None, gw, bq), lambda b, g, i: (b, g, i))]
    args = [qt]
    chunks = []
    for k, vt in sources:
        ln = k.shape[1]
        ck = min(tk, ln)
        chunks.append((ln // ck, ck))
        in_specs.append(pl.BlockSpec((None, ln, KV_WIDTH), lambda b, g, i: (b, 0, 0)))
        in_specs.append(pl.BlockSpec((None, V_ROWS, ln), lambda b, g, i: (b, g, 0)))
        args += [k, vt]
    out_shape = [jax.ShapeDtypeStruct((batch * seq, ATTN_WIDTH), BF16)]
    out_specs = [pl.BlockSpec((bq, gw), lambda b, g, i: (b * nq + i, g))]
    for w, c0, c1 in cast:
        rows = w.shape[1] // n_steps
        assert rows * n_steps == w.shape[1] and rows % BF16_SUBLANES == 0, w.shape
        in_specs.append(pl.BlockSpec((None, rows, w.shape[2]), lambda b, g, i: (layer, step(b, g, i), 0)))
        args.append(w)
        out_shape.append(jax.ShapeDtypeStruct((w.shape[1], c1 - c0), BF16))
        out_specs.append(pl.BlockSpec((rows, c1 - c0), lambda b, g, i: (step(b, g, i), 0)))
    max_chunk = max(ck for _, ck in chunks)
    return _pallas(
        functools.partial(_attn_kernel, chunks=tuple(chunks), tq=tq, cast_cols=tuple((c0, c1) for _, c0, c1 in cast)),
        name="attention",
        out_shape=tuple(out_shape),
        grid=(batch, N_KV_HEADS, nq),
        in_specs=in_specs,
        out_specs=tuple(out_specs),
        args=args,
        scratch=[pltpu.VMEM((q_tiles * GQA_GROUP, max_chunk, tq), F32)] * 2,
        temporaries=[((max_chunk, tq), F32), ((max_chunk, tq), BF16), ((V_ROWS, tq), F32)] * (q_tiles * GQA_GROUP),
    )


def _merge_mlp_kernel(x_ref, mod_ref, g1_ref, g2_ref, wg_ref, bz_ref, u_ref, up_ref, un_ref, wconv_ref,
                      yf_ref, ya_ref, wc_ref, wf_ref, wa_ref, wo_ref, w1_ref, w2_ref, o_ref, *, tps, sub, ff_chunk):
    tm = x_ref.shape[0]

    pos = pl.program_id(0) % tps
    u = u_ref[...].astype(F32)
    prev_row = jnp.where(pos == 0, 0.0, up_ref[...].astype(F32)[BF16_SUBLANES - 1:BF16_SUBLANES, :])
    next_row = jnp.where(pos == tps - 1, 0.0, un_ref[...].astype(F32)[0:1, :])
    row = lax.broadcasted_iota(jnp.int32, u.shape, 0)
    u_m1 = jnp.where(row == 0, prev_row, pltpu.roll(u, 1, 0))
    u_p1 = jnp.where(row == tm - 1, next_row, pltpu.roll(u, tm - 1, 0))
    conv = wconv_ref[0:1, :] * u_m1 + wconv_ref[1:2, :] * u + wconv_ref[2:3, :] * u_p1
    y_conv = (bz_ref[...].astype(F32) * conv).astype(BF16)

    d = D_MODEL

    def chain(r0):
        rows = slice(r0, r0 + sub)
        x = x_ref[rows, :]
        h = _modulate(x, g1_ref[...], mod_ref[0:1, :], mod_ref[1:2, :]).astype(BF16)
        yield
        m = _sigmoid(_dot(h, wg_ref[:, 0:d])) * _dot(y_conv[rows, :], wc_ref[...])
        yield
        m = m + _sigmoid(_dot(h, wg_ref[:, d:2 * d])) * _dot(yf_ref[rows, :], wf_ref[...])
        yield
        m = m + _sigmoid(_dot(h, wg_ref[:, 2 * d:3 * d])) * _dot(ya_ref[rows, :], wa_ref[...])
        yield
        x1 = x + mod_ref[2:3, :] * _dot(m.astype(BF16), wo_ref[...])
        h2 = _modulate(x1, g2_ref[...], mod_ref[3:4, :], mod_ref[4:5, :]).astype(BF16)
        acc = jnp.zeros(x1.shape, F32)
        for c in range(D_FF // ff_chunk):
            yield
            a = jnp.maximum(_dot(h2, w1_ref[:, c * ff_chunk:(c + 1) * ff_chunk]), 0.0)
            acc = acc + _dot((a * a).astype(BF16), w2_ref[c * ff_chunk:(c + 1) * ff_chunk, :])
        o_ref[rows, :] = x1 + mod_ref[5:6, :] * acc

    chains = [chain(r0) for r0 in range(0, tm, sub)]
    live = set(range(len(chains)))
    tick = 0
    while live:
        for i in sorted(live):
            if tick >= i * MERGE_LAG and next(chains[i], "done") == "done":
                live.discard(i)
        tick += 1


def _merge_mlp_call(x2, mod, g1, g2, wg, bz, u, wconv, yf, ya, wc, wf, wa, wo, w1, w2, *,
                    layer, batch, seq, tm, shared_mod):
    n = batch * seq
    tps = seq // tm
    hb = tm // BF16_SUBLANES
    n_hb = n // BF16_SUBLANES
    mod_idx = (lambda i: (0, 0, 0)) if shared_mod else (lambda i: (i // tps, 0, 0))
    const = lambda i: (0, 0)
    tile = lambda w: pl.BlockSpec((tm, w), lambda i: (i, 0))
    def weight(w):
        if w.ndim == 2:
            return pl.BlockSpec(w.shape, const, pipeline_mode=pl.Buffered(1))
        return pl.BlockSpec((None,) + w.shape[1:], lambda i: (layer, 0, 0), pipeline_mode=pl.Buffered(1))
    sub = min(tm, MERGE_SUB)
    return _pallas(
        functools.partial(_merge_mlp_kernel, tps=tps, sub=sub, ff_chunk=FF_CHUNK),
        name="merge_mlp",
        out_shape=jax.ShapeDtypeStruct((n, D_MODEL), F32),
        grid=(n // tm,),
        in_specs=[
            tile(D_MODEL),
            pl.BlockSpec((None, N_MOD, D_MODEL), mod_idx),
            pl.BlockSpec((1, D_MODEL), const),
            pl.BlockSpec((1, D_MODEL), const),
            weight(wg),
            tile(CONV_WIDTH),
            tile(CONV_WIDTH),
            pl.BlockSpec((BF16_SUBLANES, CONV_WIDTH), lambda i: (jnp.maximum(i * hb - 1, 0), 0)),
            pl.BlockSpec((BF16_SUBLANES, CONV_WIDTH), lambda i: (jnp.minimum((i + 1) * hb, n_hb - 1), 0)),
            pl.BlockSpec((CONV_K, CONV_WIDTH), const),
            tile(FOURIER_WIDTH),
            tile(ATTN_WIDTH),
            weight(wc),
            weight(wf),
            weight(wa),
            weight(wo),
            weight(w1),
            weight(w2),
        ],
        out_specs=tile(D_MODEL),
        args=(x2, mod, g1, g2, wg, bz, u, u, u, wconv, yf, ya, wc, wf, wa, wo, w1, w2),
        temporaries=[((tm, CONV_WIDTH), F32)] * 4
        + ([((sub, D_MODEL), F32)] * 8 + [((sub, FF_CHUNK), F32), ((sub, FF_CHUNK), BF16)]) * (tm // sub),
    )


def _rope_tables(rows):
    n_freq = ROPE_HALF // 2
    inv = ROPE_THETA ** (-jnp.arange(n_freq, dtype=F32) / n_freq)
    row_ang = jnp.arange(rows, dtype=F32)[:, None] * inv
    col_ang = jnp.arange(GRID_W, dtype=F32)[:, None] * inv
    expand = lambda r, c: jnp.concatenate([jnp.repeat(r, GRID_W, axis=0), jnp.tile(c, (rows, 1))], axis=-1)
    cos = expand(jnp.cos(row_ang), jnp.cos(col_ang))
    sin = expand(jnp.sin(row_ang), jnp.sin(col_ang))
    cos_h = jnp.concatenate([cos, cos], axis=-1)
    sin_h = jnp.concatenate([-sin, sin], axis=-1)
    reps = LANES // HEAD_DIM
    return jnp.tile(cos_h, (1, reps)), jnp.tile(sin_h, (1, reps))


def _channel_dft_matrix():
    c, s = _dft_tables(FOURIER_GROUP_DIM)
    eye = jnp.eye(FOURIER_GROUPS, dtype=F32)
    return jnp.concatenate([jnp.kron(eye, c), jnp.kron(eye, -s)], axis=1).astype(BF16)


def kernel(x, c, ctx, c_ctx, w_mod, b_mod, g_norm1, g_norm2, w_in, w_conv, g_q, g_k,
           w_conv_out, w_four_out, w_attn_out, w_o, w_ff1, w_ff2):
    batch, seq, d = x.shape
    ctx_len = ctx.shape[1]
    depth = w_mod.shape[0]
    rows = seq // GRID_W

    cc = jnp.zeros((MOD_ROWS, d), F32).at[:batch].set(c).at[batch].set(c_ctx)
    mods = _mod_call(cc, w_mod, b_mod)

    cos_t, sin_t = _rope_tables(rows)
    cos_c = jnp.ones((ctx_len, LANES), F32)
    sin_c = jnp.zeros((ctx_len, LANES), F32)
    w64 = _channel_dft_matrix()
    w1, wc, wf, wa, wo = _cast_weights(w_in, w_conv_out, w_four_out, w_attn_out, w_o)
    big_weights = ((w_in, OFF_G, w_in.shape[2]), (w_ff1, 0, w_ff1.shape[2]), (w_ff2, 0, w_ff2.shape[2]))

    xs = x.reshape(batch * seq, d)
    cs = ctx.reshape(batch * ctx_len, d)
    for l in range(depth):
        mod_x = mods[l, :batch].reshape(batch, N_MOD, d)
        mod_c = mods[l, batch:batch + 1].reshape(1, N_MOD, d)
        g1 = g_norm1[l].reshape(1, d)
        g2 = g_norm2[l].reshape(1, d)
        gqk = jnp.concatenate([jnp.tile(g_q[l], N_Q_HEADS), jnp.tile(g_k[l], N_KV_HEADS)]).reshape(1, QK_WIDTH)
        last = l == depth - 1

        bz_c, u_c, fab_c, qt_c, k_c, vt_c = _in_proj_call(
            cs, mod_c, g1, w1, gqk, cos_c, sin_c, w64,
            layer=l, batch=batch, seq=ctx_len, tm=ctx_len, shared_mod=True)
        k_c3 = k_c.reshape(batch, ctx_len, KV_WIDTH)

        bz, u, fab, qt, k, vt = _in_proj_call(
            xs, mod_x, g1, w1, gqk, cos_t, sin_t, w64,
            layer=l, batch=batch, seq=seq, tm=IN_PROJ_TM, shared_mod=False)
        ya, wg, wf1, wf2 = _attn_call(qt, [(k.reshape(batch, seq, KV_WIDTH), vt), (k_c3, vt_c)],
                                      batch=batch, seq=seq, tq=ATTN_TQ, tk=ATTN_TK, q_tiles=ATTN_Q_TILES,
                                      cast=big_weights, layer=l)
        yf = _fourier_mix(fab, batch=batch, seq=seq)
        xs = _merge_mlp_call(xs, mod_x, g1, g2, wg, bz, u, w_conv[l], yf, ya, wc, wf, wa, wo, wf1, wf2,
                             layer=l, batch=batch, seq=seq, tm=MERGE_TM, shared_mod=False)

        if not last:
            (ya_c,) = _attn_call(qt_c, [(k_c3, vt_c)], batch=batch, seq=ctx_len, tq=ATTN_TQ, tk=ATTN_TK,
                                 q_tiles=ctx_len // ATTN_TQ)
            yf_c = _fourier_mix(fab_c, batch=batch, seq=ctx_len)
            cs = _merge_mlp_call(cs, mod_c, g1, g2, wg, bz_c, u_c, w_conv[l], yf_c, ya_c, wc, wf, wa, wo, wf1, wf2,
                                 layer=l, batch=batch, seq=ctx_len, tm=ctx_len, shared_mod=True)
    return xs.reshape(batch, seq, d)
```
